```python
import math
import jax, jax.numpy as jnp
from jax import lax
import numpy as np

D_MODEL = 1024
BATCH = 32
SEQ = 2048
DEPTH = 2
DEC_BATCH = 8
DEC_SEQ = 2048
PAST_LEN = 128

N_META = 16
GRID_W = 64
BLOCK = 128
PAD = BLOCK - N_META
RET_HEADS = 4
RET_QK_DIM = 128
RET_V_DIM = 256
ATT_HEADS = 8
ATT_KV_HEADS = 2
ATT_HEAD_DIM = 128
D_FF = ((8 * D_MODEL + 3 * 256 - 1) // (3 * 256)) * 256
ROPE_BASE = 10000.0
NORM_EPS = 1e-6
RET_QK_W = RET_HEADS * RET_QK_DIM
RET_V_W = RET_HEADS * RET_V_DIM
ATT_Q_W = ATT_HEADS * ATT_HEAD_DIM
ATT_KV_W = ATT_KV_HEADS * ATT_HEAD_DIM
IN_SPLITS = (RET_QK_W, RET_QK_W, RET_V_W, RET_V_W, ATT_Q_W, ATT_KV_W, ATT_KV_W, D_MODEL, D_MODEL)
IN_WIDTH = RET_QK_W * 2 + RET_V_W * 2 + ATT_Q_W + ATT_KV_W * 2 + D_MODEL * 2

kernel_name = "hybrid_retention_gqa_encoder"


def _rms(x, gain=None):
    xf = x.astype(jnp.float32)
    y = xf * lax.rsqrt(jnp.mean(xf * xf, axis=-1, keepdims=True) + NORM_EPS)
    if gain is not None:
        y = y * gain.astype(jnp.float32)
    return y.astype(x.dtype)


def _rope(x, ang):
    half = x.shape[-1] // 2
    x1, x2 = x[..., :half], x[..., half:]
    c = jnp.cos(ang)[:, None, :].astype(x.dtype)
    s = jnp.sin(ang)[:, None, :].astype(x.dtype)
    return jnp.concatenate([x1 * c - x2 * s, x1 * s + x2 * c], axis=-1)


def _axial_rope(x, row_ang, col_ang):
    half = x.shape[-1] // 2
    return jnp.concatenate([_rope(x[..., :half], row_ang), _rope(x[..., half:], col_ang)], axis=-1)


def _inter_scan(qc, kc, vc, log_g, reverse):
    idx = jnp.arange(BLOCK, dtype=jnp.float32)
    if reverse:
        q_pow, k_pow = BLOCK - idx, idx
    else:
        q_pow, k_pow = idx + 1.0, BLOCK - 1.0 - idx
    q_dec = jnp.exp(log_g[:, None] * q_pow)[None, :, :, None]
    k_dec = jnp.exp(log_g[:, None] * k_pow)[None, :, :, None]
    chunk_dec = jnp.exp(log_g * BLOCK)[None, :, None, None]

    def step(state, inp):
        qi, ki, vi = inp
        out = jnp.einsum('bhid,bhdv->bhiv', qi * q_dec, state)
        state = state * chunk_dec + jnp.einsum('bhjd,bhjv->bhdv', ki * k_dec, vi)
        return state, out

    init = jnp.zeros((qc.shape[1], qc.shape[2], qc.shape[-1], vc.shape[-1]), jnp.float32)
    _, out = lax.scan(step, init, (qc, kc, vc), reverse=reverse)
    return out


def _retention_bidir(q, k, v, gamma):
    b, Lp, h, _ = q.shape
    nc = Lp // BLOCK

    def chunks(t):
        return t.reshape(b, nc, BLOCK, h, t.shape[-1]).transpose(1, 0, 3, 2, 4)

    qc, kc, vc = chunks(q), chunks(k), chunks(v)
    log_f, log_b = jnp.log(gamma[0]), jnp.log(gamma[1])
    idx = jnp.arange(BLOCK, dtype=jnp.float32)
    rel = idx[:, None] - idx[None, :]
    d_f = jnp.where(rel >= 0, jnp.exp(log_f[:, None, None] * jnp.maximum(rel, 0.0)), 0.0)
    d_b = jnp.where(rel < 0, jnp.exp(log_b[:, None, None] * jnp.maximum(-rel, 0.0)), 0.0)
    scores = jnp.einsum('cbhid,cbhjd->cbhij', qc, kc) * (d_f + d_b)
    out = jnp.einsum('cbhij,cbhjv->cbhiv', scores, vc)
    out = out + _inter_scan(qc, kc, vc, log_f, False) + _inter_scan(qc, kc, vc, log_b, True)
    return out.transpose(1, 0, 3, 2, 4).reshape(b, Lp, h, v.shape[-1])


def _block_attention(q, k, v):
    b, Lq, H, d = q.shape
    grp = H // ATT_KV_HEADS
    qb = q.reshape(b, Lq // BLOCK, BLOCK, ATT_KV_HEADS, grp, d).transpose(1, 0, 3, 4, 2, 5)
    scale = d ** -0.5

    def one_block(qi):
        s = jnp.einsum('bkgid,bjkd->bkgij', qi, k).astype(jnp.float32) * scale
        p = jax.nn.softmax(s, axis=-1).astype(v.dtype)
        return jnp.einsum('bkgij,bjkd->bkgid', p, v)

    o = lax.map(one_block, qb)
    return o.transpose(1, 0, 4, 2, 3, 5).reshape(b, Lq, H * d)


def _mixer(h, w_in, ret_decay, q_gain, k_gain, w_ret_o, w_att_o, w_out, ret_ang, row_ang, col_ang):
    b, L, _ = h.shape
    split_idx = tuple(int(i) for i in np.cumsum(IN_SPLITS)[:-1])
    proj = h @ w_in
    rq, rk, rv, rg, aq, ak, av, gr, ga = jnp.split(proj, split_idx, axis=-1)
    pad = ((0, 0), (PAD, 0), (0, 0), (0, 0))
    rq = _rope(rq.reshape(b, L, RET_HEADS, RET_QK_DIM), ret_ang).astype(jnp.float32)
    rk = _rope(rk.reshape(b, L, RET_HEADS, RET_QK_DIM), ret_ang).astype(jnp.float32) * (RET_QK_DIM ** -0.5)
    rv = rv.reshape(b, L, RET_HEADS, RET_V_DIM).astype(jnp.float32)
    gamma = 1.0 - jnp.exp2(-ret_decay.astype(jnp.float32))
    ret = _retention_bidir(jnp.pad(rq, pad), jnp.pad(rk, pad), jnp.pad(rv, pad), gamma)[:, PAD:]
    ret = _rms(ret).reshape(b, L, RET_V_W).astype(h.dtype)
    ret = (ret * jax.nn.silu(rg)) @ w_ret_o
    aq = _axial_rope(_rms(aq.reshape(b, L, ATT_HEADS, ATT_HEAD_DIM), q_gain), row_ang, col_ang)
    ak = _axial_rope(_rms(ak.reshape(b, L, ATT_KV_HEADS, ATT_HEAD_DIM), k_gain), row_ang, col_ang)
    av = av.reshape(b, L, ATT_KV_HEADS, ATT_HEAD_DIM)
    att = _block_attention(jnp.pad(aq, pad), ak, av)[:, PAD:]
    att = att @ w_att_o
    merged = jax.nn.sigmoid(gr) * ret + jax.nn.sigmoid(ga) * att
    return merged @ w_out


def _swiglu(h, w_ffn_in, w_ffn_out):
    a, u = jnp.split(h @ w_ffn_in, 2, axis=-1)
    return (jax.nn.silu(a) * u) @ w_ffn_out


def _trunk(x, meta_tokens, norm_mix, w_in, ret_decay, q_norm, k_norm, w_ret_o, w_att_o, w_out,
           norm_ffn, w_ffn_in, w_ffn_out, norm_final):
    b, S, _ = x.shape
    rows = S // GRID_W
    L = N_META + S
    h = jnp.concatenate([jnp.broadcast_to(meta_tokens.astype(x.dtype)[None], (b, N_META, D_MODEL)), x], axis=1)
    ret_inv = ROPE_BASE ** (-jnp.linspace(0.0, 1.0, RET_QK_DIM // 2, dtype=jnp.float32))
    ret_ang = jnp.arange(L, dtype=jnp.float32)[:, None] * ret_inv[None, :]
    zeros = jnp.zeros((N_META,), jnp.float32)
    row_ids = jnp.concatenate([zeros, jnp.repeat(jnp.arange(rows, dtype=jnp.float32), GRID_W)])
    col_ids = jnp.concatenate([zeros, jnp.tile(jnp.arange(GRID_W, dtype=jnp.float32), rows)])
    ax_half = ATT_HEAD_DIM // 2
    ax_inv = ROPE_BASE ** (-jnp.arange(ax_half // 2, dtype=jnp.float32) * 2.0 / ax_half)
    row_ang = row_ids[:, None] * ax_inv[None, :]
    col_ang = col_ids[:, None] * ax_inv[None, :]
    for l in range(DEPTH):
        h = h + _mixer(_rms(h, norm_mix[l]), w_in[l], ret_decay[l], q_norm[l], k_norm[l],
                       w_ret_o[l], w_att_o[l], w_out[l], ret_ang, row_ang, col_ang)
        h = h + _swiglu(_rms(h, norm_ffn[l]), w_ffn_in[l], w_ffn_out[l])
    return _rms(h, norm_final)[:, N_META:]


def setup_inputs(seed: int = 0) -> dict:
    key = jax.random.key(seed)
    ks = jax.random.split(key, 20)
    nrm = jax.random.normal
    base_decay = 5.0 + jnp.arange(RET_HEADS, dtype=jnp.float32)
    return {
        'x_prompt': nrm(ks[0], (BATCH, SEQ, D_MODEL), jnp.float32),
        'x_sample': nrm(ks[1], (DEC_BATCH, DEC_SEQ, D_MODEL), jnp.float32),
        'meta_tokens': nrm(ks[2], (N_META, D_MODEL), jnp.float32),
        'norm_mix': 1.0 + 0.02 * nrm(ks[3], (DEPTH, D_MODEL), jnp.float32),
        'w_in': nrm(ks[4], (DEPTH, D_MODEL, IN_WIDTH), jnp.float32) * D_MODEL ** -0.5,
        'ret_decay': base_decay[None, None, :] + 0.1 * nrm(ks[5], (DEPTH, 2, RET_HEADS), jnp.float32),
        'q_norm': 1.0 + 0.02 * nrm(ks[6], (DEPTH, ATT_HEAD_DIM), jnp.float32),
        'k_norm': 1.0 + 0.02 * nrm(ks[7], (DEPTH, ATT_HEAD_DIM), jnp.float32),
        'w_ret_o': nrm(ks[8], (DEPTH, RET_V_W, D_MODEL), jnp.float32) * RET_V_W ** -0.5,
        'w_att_o': nrm(ks[9], (DEPTH, ATT_Q_W, D_MODEL), jnp.float32) * ATT_Q_W ** -0.5,
        'w_out': nrm(ks[10], (DEPTH, D_MODEL, D_MODEL), jnp.float32) * D_MODEL ** -0.5,
        'norm_ffn': 1.0 + 0.02 * nrm(ks[11], (DEPTH, D_MODEL), jnp.float32),
        'w_ffn_in': nrm(ks[12], (DEPTH, D_MODEL, 2 * D_FF), jnp.float32) * D_MODEL ** -0.5,
        'w_ffn_out': nrm(ks[13], (DEPTH, D_FF, D_MODEL), jnp.float32) * D_FF ** -0.5,
        'norm_final': 1.0 + 0.02 * nrm(ks[14], (D_MODEL,), jnp.float32),
    }


def reference(x_prompt, x_sample, meta_tokens, norm_mix, w_in, ret_decay, q_norm, k_norm, w_ret_o,
              w_att_o, w_out, norm_ffn, w_ffn_in, w_ffn_out, norm_final):
    y_prompt = _trunk(x_prompt, meta_tokens, norm_mix, w_in, ret_decay, q_norm, k_norm, w_ret_o,
                      w_att_o, w_out, norm_ffn, w_ffn_in, w_ffn_out, norm_final)
    y_sample = _trunk(x_sample, meta_tokens, norm_mix, w_in, ret_decay, q_norm, k_norm, w_ret_o,
                      w_att_o, w_out, norm_ffn, w_ffn_in, w_ffn_out, norm_final)
    return (y_prompt, y_sample)
```

```python
import functools

import jax
import jax.numpy as jnp
from jax import lax
from jax.experimental import pallas as pl
from jax.experimental.pallas import tpu as pltpu

D_MODEL = 1024
DEPTH = 2
N_META = 16
GRID_W = 64
BLOCK = 128
PAD = BLOCK - N_META
RET_HEADS = 4
RET_QK_DIM = 128
RET_V_DIM = 256
ATT_HEADS = 8
ATT_KV_HEADS = 2
ATT_GROUP = ATT_HEADS // ATT_KV_HEADS
ATT_HEAD_DIM = 128
D_FF = ((8 * D_MODEL + 3 * 256 - 1) // (3 * 256)) * 256
ROPE_BASE = 10000.0
NORM_EPS = 1e-6
RET_QK_W = RET_HEADS * RET_QK_DIM
RET_V_W = RET_HEADS * RET_V_DIM
ATT_Q_W = ATT_HEADS * ATT_HEAD_DIM
ATT_KV_W = ATT_KV_HEADS * ATT_HEAD_DIM
ATT_GROUP_W = ATT_GROUP * ATT_HEAD_DIM
IN_WIDTH = RET_QK_W * 2 + RET_V_W * 2 + ATT_Q_W + ATT_KV_W * 2 + D_MODEL * 2

LANES = 128
BF16_SUBLANES = 16
VMEM_BYTES_V7X = 64 * 1024 * 1024
VMEM_LIMIT = VMEM_BYTES_V7X * 7 // 8
ROW_TILE_TARGET = 544
MASK_VALUE = -1e30

F32 = jnp.float32
BF16 = jnp.bfloat16

TAB_RC, TAB_RS, TAB_AC, TAB_ASP, TAB_ASM = range(5)
TAB_W = 5 * LANES


def _row_tile(lp, target=ROW_TILE_TARGET):
    best = None
    for d in range(BF16_SUBLANES, min(lp, target) + 1, BF16_SUBLANES):
        if lp % d == 0:
            best = d
    assert best is not None, lp
    return best


def _params(n_axes):
    return pltpu.CompilerParams(dimension_semantics=("arbitrary",) * n_axes,
                                vmem_limit_bytes=VMEM_LIMIT)


def _resident(shape):
    return pl.BlockSpec(shape, lambda *_: (0,) * len(shape), pipeline_mode=pl.Buffered(1))


def _rms_rows(x):
    return x * lax.rsqrt(jnp.mean(x * x, axis=-1, keepdims=True) + NORM_EPS)


def _sigmoid(x):
    return 1.0 / (1.0 + jnp.exp(-x))


def _rotary_tables(lp):
    t = jnp.maximum(jnp.arange(lp, dtype=jnp.int32) - PAD, 0)
    tf = t.astype(F32)
    ret_inv = ROPE_BASE ** (-jnp.linspace(0.0, 1.0, RET_QK_DIM // 2, dtype=F32))
    ret_ang = tf[:, None] * ret_inv[None, :]
    rc, rs = jnp.cos(ret_ang), jnp.sin(ret_ang)
    tok = jnp.maximum(t - N_META, 0)
    row = jnp.where(t < N_META, 0, tok // GRID_W).astype(F32)
    col = jnp.where(t < N_META, 0, tok % GRID_W).astype(F32)
    ax_half = ATT_HEAD_DIM // 2
    ax_inv = ROPE_BASE ** (-jnp.arange(ax_half // 2, dtype=F32) * 2.0 / ax_half)
    ra, ca = row[:, None] * ax_inv[None, :], col[:, None] * ax_inv[None, :]
    z = jnp.zeros_like(ra)
    return jnp.concatenate([
        rc, rc,
        -rs, rs,
        jnp.cos(ra), jnp.cos(ra), jnp.cos(ca), jnp.cos(ca),
        -jnp.sin(ra), z, -jnp.sin(ca), z,
        z, jnp.sin(ra), z, jnp.sin(ca),
    ], axis=1)


def _in_proj_kernel(h_ref, gain_ref, w_ref, tab_ref, qg_ref, kg_ref,
                    rq_ref, rk_ref, rv_ref, rg_ref, aq_ref, ak_ref, av_ref, gr_ref, ga_ref):
    xn = (_rms_rows(h_ref[0]) * gain_ref[...]).astype(BF16)

    def tab(i):
        return tab_ref[:, i * LANES:(i + 1) * LANES]

    def proj(lo, width):
        return jnp.dot(xn, w_ref[:, lo:lo + width], preferred_element_type=F32)

    def ret_rope(x, scale):
        y = x * tab(TAB_RC) + pltpu.roll(x, RET_QK_DIM // 2, 1) * tab(TAB_RS)
        return y if scale is None else y * scale

    def axial(x, gain, scale):
        y = _rms_rows(x) * gain
        y = (y * tab(TAB_AC) + pltpu.roll(y, LANES - 32, 1) * tab(TAB_ASP)
             + pltpu.roll(y, 32, 1) * tab(TAB_ASM))
        return y if scale is None else y * scale

    def heads(x, out_ref, fn):
        for hd in range(x.shape[1] // LANES):
            sl = slice(hd * LANES, (hd + 1) * LANES)
            out_ref[0, :, sl] = fn(x[:, sl]).astype(out_ref.dtype)

    lo = 0
    heads(proj(lo, RET_QK_W), rq_ref, lambda x: ret_rope(x, None))
    lo += RET_QK_W
    heads(proj(lo, RET_QK_W), rk_ref, lambda x: ret_rope(x, RET_QK_DIM ** -0.5))
    lo += RET_QK_W
    rv_ref[0] = proj(lo, RET_V_W).astype(rv_ref.dtype)
    lo += RET_V_W
    rg_ref[0] = proj(lo, RET_V_W).astype(rg_ref.dtype)
    lo += RET_V_W
    heads(proj(lo, ATT_Q_W), aq_ref, lambda x: axial(x, qg_ref[...], ATT_HEAD_DIM ** -0.5))
    lo += ATT_Q_W
    heads(proj(lo, ATT_KV_W), ak_ref, lambda x: axial(x, kg_ref[...], None))
    lo += ATT_KV_W
    av_ref[0] = proj(lo, ATT_KV_W).astype(av_ref.dtype)
    lo += ATT_KV_W
    gr_ref[0] = proj(lo, D_MODEL).astype(gr_ref.dtype)
    lo += D_MODEL
    ga_ref[0] = proj(lo, D_MODEL).astype(ga_ref.dtype)


def _in_proj(h, gain, w_in, tabs, q_gain, k_gain):
    b, lp, _ = h.shape
    tm = _row_tile(lp)
    widths = (RET_QK_W, RET_QK_W, RET_V_W, RET_V_W, ATT_Q_W, ATT_KV_W, ATT_KV_W, D_MODEL, D_MODEL)

    def rows(w):
        return pl.BlockSpec((1, tm, w), lambda i, j: (i, j, 0))

    return pl.pallas_call(
        _in_proj_kernel,
        grid=(b, lp // tm),
        in_specs=[rows(D_MODEL), _resident((1, D_MODEL)), _resident((D_MODEL, IN_WIDTH)),
                  pl.BlockSpec((tm, TAB_W), lambda i, j: (j, 0)),
                  _resident((1, LANES)), _resident((1, LANES))],
        out_specs=[rows(w) for w in widths],
        out_shape=[jax.ShapeDtypeStruct((b, lp, w), BF16) for w in widths],
        compiler_params=_params(2),
        name="in_proj",
    )(h, gain, w_in, tabs, q_gain, k_gain)


def _retention_kernel(lg_ref, q_ref, k_ref, v_ref, g_ref, o_ref, acc_ref):
    hd = pl.program_id(1)
    log_f, log_b = lg_ref[0, hd], lg_ref[1, hd]
    nc = q_ref.shape[1] // BLOCK

    ii = lax.broadcasted_iota(jnp.int32, (BLOCK, BLOCK), 0)
    jj = lax.broadcasted_iota(jnp.int32, (BLOCK, BLOCK), 1)
    rel = (ii - jj).astype(F32)
    decay = jnp.where(rel >= 0, jnp.exp(log_f * jnp.maximum(rel, 0.0)),
                      jnp.exp(log_b * jnp.maximum(-rel, 0.0)))
    idx = lax.broadcasted_iota(jnp.int32, (BLOCK, 1), 0).astype(F32)
    q_dec_f, k_dec_f = jnp.exp(log_f * (idx + 1.0)), jnp.exp(log_f * (BLOCK - 1.0 - idx))
    q_dec_b, k_dec_b = jnp.exp(log_b * (BLOCK - idx)), jnp.exp(log_b * idx)
    ones_row = jnp.ones((1, RET_V_DIM), F32)
    chunk_dec_f, chunk_dec_b = jnp.exp(log_f * BLOCK * ones_row), jnp.exp(log_b * BLOCK * ones_row)

    def chunk(c):
        r = pl.ds(pl.multiple_of(c * BLOCK, BLOCK), BLOCK)
        return r, q_ref[0, r, :], k_ref[0, r, :], v_ref[0, r, :]

    def read_state(q, q_dec, state):
        return jnp.dot((q.astype(F32) * q_dec).astype(BF16), state.astype(BF16),
                       preferred_element_type=F32)

    def push_state(k, v, k_dec, chunk_dec, state):
        kd = (k.astype(F32) * k_dec).astype(BF16)
        return state * chunk_dec + lax.dot_general(kd, v, (((0,), (0,)), ((), ())),
                                                   preferred_element_type=F32)

    def forward(c, state):
        r, q, k, v = chunk(c)
        s = lax.dot_general(q, k, (((1,), (1,)), ((), ())), preferred_element_type=F32) * decay
        intra = jnp.dot(s.astype(BF16), v, preferred_element_type=F32)
        acc_ref[r, :] = intra + read_state(q, q_dec_f, state)
        return push_state(k, v, k_dec_f, chunk_dec_f, state)

    def backward(t, state):
        r, q, k, v = chunk(nc - 1 - t)
        o = _rms_rows(acc_ref[r, :] + read_state(q, q_dec_b, state))
        g = g_ref[0, r, :].astype(F32)
        o_ref[0, r, :] = (o * (g * _sigmoid(g))).astype(o_ref.dtype)
        return push_state(k, v, k_dec_b, chunk_dec_b, state)

    zero = jnp.zeros((RET_QK_DIM, RET_V_DIM), F32)
    lax.fori_loop(0, nc, forward, zero)
    lax.fori_loop(0, nc, backward, zero)


def _retention(log_gamma, rq, rk, rv, rg):
    b, lp, _ = rq.shape

    def head(w):
        return pl.BlockSpec((1, lp, w), lambda i, j, lg: (i, 0, j))

    return pl.pallas_call(
        _retention_kernel,
        grid_spec=pltpu.PrefetchScalarGridSpec(
            num_scalar_prefetch=1,
            grid=(b, RET_HEADS),
            in_specs=[head(RET_QK_DIM), head(RET_QK_DIM), head(RET_V_DIM), head(RET_V_DIM)],
            out_specs=head(RET_V_DIM),
            scratch_shapes=[pltpu.VMEM((lp, RET_V_DIM), F32)],
        ),
        out_shape=jax.ShapeDtypeStruct((b, lp, RET_V_W), BF16),
        compiler_params=_params(2),
        name="retention",
    )(log_gamma, rq, rk, rv, rg)


def _attention_kernel(q_ref, k_ref, v_ref, o_ref):
    tq = q_ref.shape[1]
    lp = k_ref.shape[1]
    q = jnp.concatenate([q_ref[0, :, g * LANES:(g + 1) * LANES] for g in range(ATT_GROUP)], axis=0)
    s = lax.dot_general(q, k_ref[0], (((1,), (1,)), ((), ())), preferred_element_type=F32)
    key = lax.broadcasted_iota(jnp.int32, (1, lp), 1)
    s = jnp.where(key >= PAD, s, MASK_VALUE)
    p = jnp.exp(s - jnp.max(s, axis=-1, keepdims=True))
    denom = jnp.sum(p, axis=-1, keepdims=True)
    o = jnp.dot(p.astype(BF16), v_ref[0], preferred_element_type=F32) * (1.0 / denom)
    for g in range(ATT_GROUP):
        o_ref[0, :, g * LANES:(g + 1) * LANES] = o[g * tq:(g + 1) * tq].astype(o_ref.dtype)


def _attention(aq, ak, av):
    b, lp, _ = aq.shape
    tq = BLOCK
    return pl.pallas_call(
        _attention_kernel,
        grid=(b, ATT_KV_HEADS, lp // tq),
        in_specs=[pl.BlockSpec((1, tq, ATT_GROUP_W), lambda i, j, t: (i, t, j)),
                  pl.BlockSpec((1, lp, ATT_HEAD_DIM), lambda i, j, t: (i, 0, j)),
                  pl.BlockSpec((1, lp, ATT_HEAD_DIM), lambda i, j, t: (i, 0, j))],
        out_specs=pl.BlockSpec((1, tq, ATT_GROUP_W), lambda i, j, t: (i, t, j)),
        out_shape=jax.ShapeDtypeStruct((b, lp, ATT_Q_W), BF16),
        compiler_params=_params(3),
        name="attention",
    )(aq, ak, av)


def _mix_out_kernel(h_ref, ret_ref, att_ref, gr_ref, ga_ref, wr_ref, wa_ref, wo_ref, o_ref):
    tm = h_ref.shape[1]
    r = jnp.dot(ret_ref[0], wr_ref[...], preferred_element_type=F32)
    a = jnp.dot(att_ref[0], wa_ref[...], preferred_element_type=F32)
    merged = _sigmoid(gr_ref[0].astype(F32)) * r + _sigmoid(ga_ref[0].astype(F32)) * a
    out = h_ref[0] + jnp.dot(merged.astype(BF16), wo_ref[...], preferred_element_type=F32)
    row = pl.program_id(1) * tm + lax.broadcasted_iota(jnp.int32, (tm, 1), 0)
    o_ref[0] = jnp.where(row >= PAD, out, 0.0)


def _mix_out(h, ret, att, gr, ga, w_ret_o, w_att_o, w_out):
    b, lp, _ = h.shape
    tm = _row_tile(lp)
    rows = pl.BlockSpec((1, tm, D_MODEL), lambda i, j: (i, j, 0))
    w = _resident((D_MODEL, D_MODEL))
    return pl.pallas_call(
        _mix_out_kernel,
        grid=(b, lp // tm),
        in_specs=[rows] * 5 + [w] * 3,
        out_specs=rows,
        out_shape=jax.ShapeDtypeStruct(h.shape, F32),
        input_output_aliases={0: 0},
        compiler_params=_params(2),
        name="mix_out",
    )(h, ret, att, gr, ga, w_ret_o, w_att_o, w_out)


def _ffn_kernel(h_ref, gain_ref, wi_ref, wo_ref, o_ref):
    x = h_ref[0]
    xn = (_rms_rows(x) * gain_ref[...]).astype(BF16)
    a = jnp.dot(xn, wi_ref[:, :D_FF], preferred_element_type=F32)
    u = jnp.dot(xn, wi_ref[:, D_FF:], preferred_element_type=F32)
    act = (a * _sigmoid(a) * u).astype(BF16)
    o_ref[0] = x + jnp.dot(act, wo_ref[...], preferred_element_type=F32)


def _ffn(h, gain, w_ffn_in, w_ffn_out):
    b, lp, _ = h.shape
    tm = _row_tile(lp)
    rows = pl.BlockSpec((1, tm, D_MODEL), lambda i, j: (i, j, 0))
    return pl.pallas_call(
        _ffn_kernel,
        grid=(b, lp // tm),
        in_specs=[rows, _resident((1, D_MODEL)), _resident((D_MODEL, 2 * D_FF)),
                  _resident((D_FF, D_MODEL))],
        out_specs=rows,
        out_shape=jax.ShapeDtypeStruct(h.shape, F32),
        input_output_aliases={0: 0},
        compiler_params=_params(2),
        name="ffn",
    )(h, gain, w_ffn_in, w_ffn_out)


def _final_norm_kernel(h_ref, gain_ref, o_ref):
    o_ref[0] = _rms_rows(h_ref[0, BLOCK:, :]) * gain_ref[...]


def _final_norm(h, gain, first, count):
    _, lp, _ = h.shape
    s = lp - BLOCK
    return pl.pallas_call(
        _final_norm_kernel,
        grid=(count,),
        in_specs=[pl.BlockSpec((1, lp, D_MODEL), lambda i: (i + first, 0, 0)),
                  _resident((1, D_MODEL))],
        out_specs=pl.BlockSpec((1, s, D_MODEL), lambda i: (i, 0, 0)),
        out_shape=jax.ShapeDtypeStruct((count, s, D_MODEL), F32),
        compiler_params=_params(1),
        name="final_norm",
    )(h, gain)


def _trunk(xs, meta_tokens, norm_mix, w_in, ret_decay, q_norm, k_norm, w_ret_o, w_att_o, w_out,
           norm_ffn, w_ffn_in, w_ffn_out, norm_final):
    s = xs[0].shape[1]
    assert all(x.shape[1:] == (s, D_MODEL) for x in xs) and s % GRID_W == 0
    lp = BLOCK + s
    counts = [x.shape[0] for x in xs]
    b = sum(counts)
    head = jnp.concatenate([jnp.zeros((PAD, D_MODEL), F32), meta_tokens.astype(F32)], axis=0)
    h = jnp.concatenate([jnp.broadcast_to(head[None], (b, BLOCK, D_MODEL)),
                         jnp.concatenate(xs, axis=0)], axis=1)
    tabs = _rotary_tables(lp)
    gamma = 1.0 - jnp.exp2(-ret_decay.astype(F32))
    log_gamma = jnp.log(gamma)
    for l in range(DEPTH):
        rq, rk, rv, rg, aq, ak, av, gr, ga = _in_proj(
            h, norm_mix[l][None], w_in[l].astype(BF16), tabs, q_norm[l][None], k_norm[l][None])
        ret = _retention(log_gamma[l], rq, rk, rv, rg)
        att = _attention(aq, ak, av)
        h = _mix_out(h, ret, att, gr, ga, w_ret_o[l].astype(BF16), w_att_o[l].astype(BF16),
                     w_out[l].astype(BF16))
        h = _ffn(h, norm_ffn[l][None], w_ffn_in[l].astype(BF16), w_ffn_out[l].astype(BF16))
    outs, first = [], 0
    for c in counts:
        outs.append(_final_norm(h, norm_final[None], first, c))
        first += c
    return outs


def kernel(x_prompt, x_sample, meta_tokens, norm_mix, w_in, ret_decay, q_norm, k_norm, w_ret_o,
           w_att_o, w_out, norm_ffn, w_ffn_in, w_ffn_out, norm_final):
    y_prompt, y_sample = _trunk([x_prompt, x_sample], meta_tokens, norm_mix, w_in, ret_decay,
                                q_norm, k_norm, w_ret_o, w_att_o, w_out, norm_ffn, w_ffn_in,
                                w_ffn_out, norm_final)
    return (y_prompt, y_sample)
```

```python
import functools

import jax
import jax.numpy as jnp
from jax import lax
from jax.experimental import pallas as pl
from jax.experimental.pallas import tpu as pltpu

D_MODEL = 1024
DEPTH = 2
N_META = 16
GRID_W = 64
BLOCK = 128
PAD = BLOCK - N_META
RET_HEADS = 4
RET_QK_DIM = 128
RET_V_DIM = 256
ATT_HEADS = 8
ATT_KV_HEADS = 2
ATT_GROUP = ATT_HEADS // ATT_KV_HEADS
ATT_HEAD_DIM = 128
D_FF = ((8 * D_MODEL + 3 * 256 - 1) // (3 * 256)) * 256
ROPE_BASE = 10000.0
NORM_EPS = 1e-6
RET_QK_W = RET_HEADS * RET_QK_DIM
RET_V_W = RET_HEADS * RET_V_DIM
ATT_Q_W = ATT_HEADS * ATT_HEAD_DIM
ATT_KV_W = ATT_KV_HEADS * ATT_HEAD_DIM
ATT_GROUP_W = ATT_GROUP * ATT_HEAD_DIM
IN_WIDTH = RET_QK_W * 2 + RET_V_W * 2 + ATT_Q_W + ATT_KV_W * 2 + D_MODEL * 2

LANES = 128
BF16_SUBLANES = 16
VMEM_BYTES_V7X = 64 * 1024 * 1024
VMEM_LIMIT = VMEM_BYTES_V7X * 7 // 8
ROW_TILE_TARGET = 544
ATT_SUB_ROWS = 272
MASK_VALUE = -1e30

F32 = jnp.float32
BF16 = jnp.bfloat16

TAB_RC, TAB_RS, TAB_AC, TAB_ASP, TAB_ASM = range(5)
TAB_W = 5 * LANES


def _row_tile(lp, target=ROW_TILE_TARGET):
    best = None
    for d in range(BF16_SUBLANES, min(lp, target) + 1, BF16_SUBLANES):
        if lp % d == 0:
            best = d
    assert best is not None, lp
    return best


def _params(n_axes):
    return pltpu.CompilerParams(dimension_semantics=("arbitrary",) * n_axes,
                                vmem_limit_bytes=VMEM_LIMIT)


def _resident(shape):
    return pl.BlockSpec(shape, lambda *_: (0,) * len(shape), pipeline_mode=pl.Buffered(1))


def _rms_rows(x):
    return x * lax.rsqrt(jnp.mean(x * x, axis=-1, keepdims=True) + NORM_EPS)


def _sigmoid(x):
    return 1.0 / (1.0 + jnp.exp(-x))


def _rotary_tables(lp):
    t = jnp.maximum(jnp.arange(lp, dtype=jnp.int32) - PAD, 0)
    tf = t.astype(F32)
    ret_inv = ROPE_BASE ** (-jnp.linspace(0.0, 1.0, RET_QK_DIM // 2, dtype=F32))
    ret_ang = tf[:, None] * ret_inv[None, :]
    rc, rs = jnp.cos(ret_ang), jnp.sin(ret_ang)
    tok = jnp.maximum(t - N_META, 0)
    row = jnp.where(t < N_META, 0, tok // GRID_W).astype(F32)
    col = jnp.where(t < N_META, 0, tok % GRID_W).astype(F32)
    ax_half = ATT_HEAD_DIM // 2
    ax_inv = ROPE_BASE ** (-jnp.arange(ax_half // 2, dtype=F32) * 2.0 / ax_half)
    ra, ca = row[:, None] * ax_inv[None, :], col[:, None] * ax_inv[None, :]
    z = jnp.zeros_like(ra)
    return jnp.concatenate([
        rc, rc,
        -rs, rs,
        jnp.cos(ra), jnp.cos(ra), jnp.cos(ca), jnp.cos(ca),
        -jnp.sin(ra), z, -jnp.sin(ca), z,
        z, jnp.sin(ra), z, jnp.sin(ca),
    ], axis=1)


def _in_proj_kernel(h_ref, gain_ref, w_ref, tab_ref, qg_ref, kg_ref,
                    rq_ref, rk_ref, rv_ref, rg_ref, aq_ref, ak_ref, av_ref, gr_ref, ga_ref):
    xn = (_rms_rows(h_ref[0]) * gain_ref[...]).astype(BF16)

    def tab(i):
        return tab_ref[:, i * LANES:(i + 1) * LANES]

    def proj(lo, width):
        return jnp.dot(xn, w_ref[:, lo:lo + width], preferred_element_type=F32)

    def ret_rope(x, scale):
        y = x * tab(TAB_RC) + pltpu.roll(x, RET_QK_DIM // 2, 1) * tab(TAB_RS)
        return y if scale is None else y * scale

    def axial(x, gain, scale):
        y = _rms_rows(x) * gain
        y = (y * tab(TAB_AC) + pltpu.roll(y, LANES - 32, 1) * tab(TAB_ASP)
             + pltpu.roll(y, 32, 1) * tab(TAB_ASM))
        return y if scale is None else y * scale

    def heads(x, out_ref, fn):
        for hd in range(x.shape[1] // LANES):
            sl = slice(hd * LANES, (hd + 1) * LANES)
            out_ref[0, :, sl] = fn(x[:, sl]).astype(out_ref.dtype)

    lo = 0
    heads(proj(lo, RET_QK_W), rq_ref, lambda x: ret_rope(x, None))
    lo += RET_QK_W
    heads(proj(lo, RET_QK_W), rk_ref, lambda x: ret_rope(x, RET_QK_DIM ** -0.5))
    lo += RET_QK_W
    rv_ref[0] = proj(lo, RET_V_W).astype(rv_ref.dtype)
    lo += RET_V_W
    rg_ref[0] = proj(lo, RET_V_W).astype(rg_ref.dtype)
    lo += RET_V_W
    heads(proj(lo, ATT_Q_W), aq_ref, lambda x: axial(x, qg_ref[...], ATT_HEAD_DIM ** -0.5))
    lo += ATT_Q_W
    heads(proj(lo, ATT_KV_W), ak_ref, lambda x: axial(x, kg_ref[...], None))
    lo += ATT_KV_W
    av_ref[0] = proj(lo, ATT_KV_W).astype(av_ref.dtype)
    lo += ATT_KV_W
    gr_ref[0] = proj(lo, D_MODEL).astype(gr_ref.dtype)
    lo += D_MODEL
    ga_ref[0] = proj(lo, D_MODEL).astype(ga_ref.dtype)


def _in_proj(h, gain, w_in, tabs, q_gain, k_gain):
    b, lp, _ = h.shape
    tm = _row_tile(lp)
    widths = (RET_QK_W, RET_QK_W, RET_V_W, RET_V_W, ATT_Q_W, ATT_KV_W, ATT_KV_W, D_MODEL, D_MODEL)

    def rows(w):
        return pl.BlockSpec((1, tm, w), lambda i, j: (i, j, 0))

    return pl.pallas_call(
        _in_proj_kernel,
        grid=(b, lp // tm),
        in_specs=[rows(D_MODEL), _resident((1, D_MODEL)), _resident((D_MODEL, IN_WIDTH)),
                  pl.BlockSpec((tm, TAB_W), lambda i, j: (j, 0)),
                  _resident((1, LANES)), _resident((1, LANES))],
        out_specs=[rows(w) for w in widths],
        out_shape=[jax.ShapeDtypeStruct((b, lp, w), BF16) for w in widths],
        compiler_params=_params(2),
        name="in_proj",
    )(h, gain, w_in, tabs, q_gain, k_gain)


COEF_DECAY, COEF_Q_FWD, COEF_Q_BWD, COEF_K_FWD, COEF_K_BWD = range(5)
FWD, BWD = 0, 1


def _retention_kernel(lg_ref, q_ref, k_ref, v_ref, g_ref, o_ref, coef_ref, state_ref, stack_ref):
    nc = q_ref.shape[1] // BLOCK
    row = lax.broadcasted_iota(jnp.int32, (BLOCK, BLOCK), 0).astype(F32)
    col = lax.broadcasted_iota(jnp.int32, (BLOCK, BLOCK), 1).astype(F32)
    rel = row - col
    chunk_decay = []
    for hd in range(RET_HEADS):
        log_f, log_b = lg_ref[FWD, hd], lg_ref[BWD, hd]
        coef_ref[hd, COEF_DECAY] = jnp.where(rel >= 0, jnp.exp(log_f * jnp.maximum(rel, 0.0)),
                                             jnp.exp(log_b * jnp.maximum(-rel, 0.0)))
        coef_ref[hd, COEF_Q_FWD] = jnp.exp(log_f * (row + 1.0))
        coef_ref[hd, COEF_Q_BWD] = jnp.exp(log_b * (BLOCK - row))
        coef_ref[hd, COEF_K_FWD] = jnp.exp(log_f * (BLOCK - 1.0 - row))
        coef_ref[hd, COEF_K_BWD] = jnp.exp(log_b * row)
        ones_row = jnp.ones((1, RET_V_DIM), F32)
        chunk_decay.append((jnp.exp(log_f * BLOCK * ones_row), jnp.exp(log_b * BLOCK * ones_row)))
    state_ref[...] = jnp.zeros_like(state_ref)

    def rows_of(c):
        return pl.ds(pl.multiple_of(c * BLOCK, BLOCK), BLOCK)

    def qk_cols(hd):
        return slice(hd * RET_QK_DIM, (hd + 1) * RET_QK_DIM)

    def v_cols(hd):
        return slice(hd * RET_V_DIM, (hd + 1) * RET_V_DIM)

    def stack_rows(c, hd, direction):
        base = (c * RET_HEADS + hd) * 2 * BLOCK + direction * BLOCK
        return pl.ds(pl.multiple_of(base, BLOCK), BLOCK)

    def scan_step(c, direction):
        r = rows_of(c)
        for hd in range(RET_HEADS):
            slot = direction * RET_HEADS + hd
            state = state_ref[slot]
            stack_ref[stack_rows(c, hd, direction), :] = state.astype(stack_ref.dtype)
            k = k_ref[0, r, qk_cols(hd)].astype(F32) * coef_ref[hd, COEF_K_FWD + direction]
            update = lax.dot_general(k.astype(BF16), v_ref[0, r, v_cols(hd)],
                                     (((0,), (0,)), ((), ())), preferred_element_type=F32)
            state_ref[slot] = state * chunk_decay[hd][direction] + update

    def scan(t, carry):
        scan_step(t, FWD)
        scan_step(nc - 1 - t, BWD)
        return carry

    def emit(c, carry):
        r = rows_of(c)
        for hd in range(RET_HEADS):
            q, k, v = q_ref[0, r, qk_cols(hd)], k_ref[0, r, qk_cols(hd)], v_ref[0, r, v_cols(hd)]
            s = lax.dot_general(q, k, (((1,), (1,)), ((), ())), preferred_element_type=F32)
            qf = q.astype(F32)
            lhs = jnp.concatenate([(s * coef_ref[hd, COEF_DECAY]).astype(BF16),
                                   (qf * coef_ref[hd, COEF_Q_FWD]).astype(BF16),
                                   (qf * coef_ref[hd, COEF_Q_BWD]).astype(BF16)], axis=1)
            states = stack_ref[pl.ds(pl.multiple_of((c * RET_HEADS + hd) * 2 * BLOCK, BLOCK),
                                     2 * BLOCK), :]
            rhs = jnp.concatenate([v, states], axis=0)
            o = _rms_rows(jnp.dot(lhs, rhs, preferred_element_type=F32))
            g = g_ref[0, r, v_cols(hd)].astype(F32)
            o_ref[0, r, v_cols(hd)] = (o * (g * _sigmoid(g))).astype(o_ref.dtype)
        return carry

    lax.fori_loop(0, nc, scan, 0)
    lax.fori_loop(0, nc, emit, 0)


def _retention(log_gamma, rq, rk, rv, rg):
    b, lp, _ = rq.shape
    nc = lp // BLOCK

    def seq(w):
        return pl.BlockSpec((1, lp, w), lambda i, lg: (i, 0, 0))

    return pl.pallas_call(
        _retention_kernel,
        grid_spec=pltpu.PrefetchScalarGridSpec(
            num_scalar_prefetch=1,
            grid=(b,),
            in_specs=[seq(RET_QK_W), seq(RET_QK_W), seq(RET_V_W), seq(RET_V_W)],
            out_specs=seq(RET_V_W),
            scratch_shapes=[pltpu.VMEM((RET_HEADS, 5, BLOCK, BLOCK), F32),
                            pltpu.VMEM((2 * RET_HEADS, RET_QK_DIM, RET_V_DIM), F32),
                            pltpu.VMEM((nc * RET_HEADS * 2 * BLOCK, RET_V_DIM), BF16)],
        ),
        out_shape=jax.ShapeDtypeStruct((b, lp, RET_V_W), BF16),
        compiler_params=_params(1),
        name="retention",
    )(log_gamma, rq, rk, rv, rg)


def _attention_kernel(q_ref, k_ref, v_ref, o_ref):
    tq = q_ref.shape[1]
    lp = k_ref.shape[1]
    sub = _row_tile(tq, ATT_SUB_ROWS)
    key = lax.broadcasted_iota(jnp.int32, (1, lp), 1)
    for r0 in range(0, tq, sub):
        for g in range(ATT_GROUP):
            rows, cols = slice(r0, r0 + sub), slice(g * LANES, (g + 1) * LANES)
            s = lax.dot_general(q_ref[0, rows, cols], k_ref[0], (((1,), (1,)), ((), ())),
                                preferred_element_type=F32)
            s = jnp.where(key >= PAD, s, MASK_VALUE)
            p = jnp.exp(s - jnp.max(s, axis=-1, keepdims=True))
            denom = jnp.sum(p, axis=-1, keepdims=True)
            o = jnp.dot(p.astype(BF16), v_ref[0], preferred_element_type=F32) * (1.0 / denom)
            o_ref[0, rows, cols] = o.astype(o_ref.dtype)


def _attention(aq, ak, av):
    b, lp, _ = aq.shape
    tq = _row_tile(lp)
    return pl.pallas_call(
        _attention_kernel,
        grid=(b, ATT_KV_HEADS, lp // tq),
        in_specs=[pl.BlockSpec((1, tq, ATT_GROUP_W), lambda i, j, t: (i, t, j)),
                  pl.BlockSpec((1, lp, ATT_HEAD_DIM), lambda i, j, t: (i, 0, j)),
                  pl.BlockSpec((1, lp, ATT_HEAD_DIM), lambda i, j, t: (i, 0, j))],
        out_specs=pl.BlockSpec((1, tq, ATT_GROUP_W), lambda i, j, t: (i, t, j)),
        out_shape=jax.ShapeDtypeStruct((b, lp, ATT_Q_W), BF16),
        compiler_params=_params(3),
        name="attention",
    )(aq, ak, av)


def _mix_out_kernel(h_ref, ret_ref, att_ref, gr_ref, ga_ref, wr_ref, wa_ref, wo_ref, o_ref):
    tm = h_ref.shape[1]
    r = jnp.dot(ret_ref[0], wr_ref[...], preferred_element_type=F32)
    a = jnp.dot(att_ref[0], wa_ref[...], preferred_element_type=F32)
    merged = _sigmoid(gr_ref[0].astype(F32)) * r + _sigmoid(ga_ref[0].astype(F32)) * a
    out = h_ref[0] + jnp.dot(merged.astype(BF16), wo_ref[...], preferred_element_type=F32)
    row = pl.program_id(1) * tm + lax.broadcasted_iota(jnp.int32, (tm, 1), 0)
    o_ref[0] = jnp.where(row >= PAD, out, 0.0)


def _mix_out(h, ret, att, gr, ga, w_ret_o, w_att_o, w_out):
    b, lp, _ = h.shape
    tm = _row_tile(lp)
    rows = pl.BlockSpec((1, tm, D_MODEL), lambda i, j: (i, j, 0))
    w = _resident((D_MODEL, D_MODEL))
    return pl.pallas_call(
        _mix_out_kernel,
        grid=(b, lp // tm),
        in_specs=[rows] * 5 + [w] * 3,
        out_specs=rows,
        out_shape=jax.ShapeDtypeStruct(h.shape, F32),
        input_output_aliases={0: 0},
        compiler_params=_params(2),
        name="mix_out",
    )(h, ret, att, gr, ga, w_ret_o, w_att_o, w_out)


def _ffn_kernel(h_ref, gain_ref, wi_ref, wo_ref, o_ref):
    x = h_ref[0]
    xn = (_rms_rows(x) * gain_ref[...]).astype(BF16)
    a = jnp.dot(xn, wi_ref[:, :D_FF], preferred_element_type=F32)
    u = jnp.dot(xn, wi_ref[:, D_FF:], preferred_element_type=F32)
    act = (a * _sigmoid(a) * u).astype(BF16)
    o_ref[0] = x + jnp.dot(act, wo_ref[...], preferred_element_type=F32)


def _ffn(h, gain, w_ffn_in, w_ffn_out):
    b, lp, _ = h.shape
    tm = _row_tile(lp)
    rows = pl.BlockSpec((1, tm, D_MODEL), lambda i, j: (i, j, 0))
    return pl.pallas_call(
        _ffn_kernel,
        grid=(b, lp // tm),
        in_specs=[rows, _resident((1, D_MODEL)), _resident((D_MODEL, 2 * D_FF)),
                  _resident((D_FF, D_MODEL))],
        out_specs=rows,
        out_shape=jax.ShapeDtypeStruct(h.shape, F32),
        input_output_aliases={0: 0},
        compiler_params=_params(2),
        name="ffn",
    )(h, gain, w_ffn_in, w_ffn_out)


def _final_norm_kernel(h_ref, gain_ref, o_ref):
    o_ref[0] = _rms_rows(h_ref[0, BLOCK:, :]) * gain_ref[...]


def _final_norm(h, gain, first, count):
    _, lp, _ = h.shape
    s = lp - BLOCK
    return pl.pallas_call(
        _final_norm_kernel,
        grid=(count,),
        in_specs=[pl.BlockSpec((1, lp, D_MODEL), lambda i: (i + first, 0, 0)),
                  _resident((1, D_MODEL))],
        out_specs=pl.BlockSpec((1, s, D_MODEL), lambda i: (i, 0, 0)),
        out_shape=jax.ShapeDtypeStruct((count, s, D_MODEL), F32),
        compiler_params=_params(1),
        name="final_norm",
    )(h, gain)


def _trunk(xs, meta_tokens, norm_mix, w_in, ret_decay, q_norm, k_norm, w_ret_o, w_att_o, w_out,
           norm_ffn, w_ffn_in, w_ffn_out, norm_final):
    s = xs[0].shape[1]
    assert all(x.shape[1:] == (s, D_MODEL) for x in xs) and s % GRID_W == 0
    lp = BLOCK + s
    counts = [x.shape[0] for x in xs]
    b = sum(counts)
    head = jnp.concatenate([jnp.zeros((PAD, D_MODEL), F32), meta_tokens.astype(F32)], axis=0)
    h = jnp.concatenate([jnp.broadcast_to(head[None], (b, BLOCK, D_MODEL)),
                         jnp.concatenate(xs, axis=0)], axis=1)
    tabs = _rotary_tables(lp)
    gamma = 1.0 - jnp.exp2(-ret_decay.astype(F32))
    log_gamma = jnp.log(gamma)
    for l in range(DEPTH):
        rq, rk, rv, rg, aq, ak, av, gr, ga = _in_proj(
            h, norm_mix[l][None], w_in[l].astype(BF16), tabs, q_norm[l][None], k_norm[l][None])
        ret = _retention(log_gamma[l], rq, rk, rv, rg)
        att = _attention(aq, ak, av)
        h = _mix_out(h, ret, att, gr, ga, w_ret_o[l].astype(BF16), w_att_o[l].astype(BF16),
                     w_out[l].astype(BF16))
        h = _ffn(h, norm_ffn[l][None], w_ffn_in[l].astype(BF16), w_ffn_out[l].astype(BF16))
    outs, first = [], 0
    for c in counts:
        outs.append(_final_norm(h, norm_final[None], first, c))
        first += c
    return outs


def kernel(x_prompt, x_sample, meta_tokens, norm_mix, w_in, ret_decay, q_norm, k_norm, w_ret_o,
           w_att_o, w_out, norm_ffn, w_ffn_in, w_ffn_out, norm_final):
    y_prompt, y_sample = _trunk([x_prompt, x_sample], meta_tokens, norm_mix, w_in, ret_decay,
                                q_norm, k_norm, w_ret_o, w_att_o, w_out, norm_ffn, w_ffn_in,
                                w_ffn_out, norm_final)
    return (y_prompt, y_sample)
```

```python
import functools

import jax
import jax.numpy as jnp
from jax import lax
from jax.experimental import pallas as pl
from jax.experimental.pallas import tpu as pltpu

D_MODEL = 1024
DEPTH = 2
N_META = 16
GRID_W = 64
BLOCK = 128
PAD = BLOCK - N_META
RET_HEADS = 4
RET_QK_DIM = 128
RET_V_DIM = 256
ATT_HEADS = 8
ATT_KV_HEADS = 2
ATT_GROUP = ATT_HEADS // ATT_KV_HEADS
ATT_HEAD_DIM = 128
D_FF = ((8 * D_MODEL + 3 * 256 - 1) // (3 * 256)) * 256
ROPE_BASE = 10000.0
NORM_EPS = 1e-6
RET_QK_W = RET_HEADS * RET_QK_DIM
RET_V_W = RET_HEADS * RET_V_DIM
ATT_Q_W = ATT_HEADS * ATT_HEAD_DIM
ATT_KV_W = ATT_KV_HEADS * ATT_HEAD_DIM
ATT_GROUP_W = ATT_GROUP * ATT_HEAD_DIM
IN_NAMES = ("rq", "rk", "rv", "rg", "aq", "ak", "av", "gr", "ga")
IN_SPLITS = (RET_QK_W, RET_QK_W, RET_V_W, RET_V_W, ATT_Q_W, ATT_KV_W, ATT_KV_W, D_MODEL, D_MODEL)
IN_WIDTH = sum(IN_SPLITS)

LANES = 128
BF16_SUBLANES = 16
VMEM_BYTES_V7X = 64 * 1024 * 1024
VMEM_LIMIT = VMEM_BYTES_V7X * 7 // 8
ROW_TILE_TARGET = 544
ATT_Q_ROWS = 1088
ATT_SUB_ROWS = 272
MASK_VALUE = -1e30

F32 = jnp.float32
BF16 = jnp.bfloat16

TAB_RC, TAB_RS, TAB_AC, TAB_ASP, TAB_ASM = range(5)
TAB_W = 5 * LANES


def _row_tile(lp, target=ROW_TILE_TARGET):
    best = None
    for d in range(BF16_SUBLANES, min(lp, target) + 1, BF16_SUBLANES):
        if lp % d == 0:
            best = d
    assert best is not None, lp
    return best


def _params(n_axes):
    return pltpu.CompilerParams(dimension_semantics=("arbitrary",) * n_axes,
                                vmem_limit_bytes=VMEM_LIMIT)


def _resident(shape):
    return pl.BlockSpec(shape, lambda *_: (0,) * len(shape), pipeline_mode=pl.Buffered(1))


def _rms_rows(x):
    return x * lax.rsqrt(jnp.mean(x * x, axis=-1, keepdims=True) + NORM_EPS)


def _sigmoid(x):
    return 1.0 / (1.0 + jnp.exp(-x))


def _rotary_tables(lp):
    t = jnp.maximum(jnp.arange(lp, dtype=jnp.int32) - PAD, 0)
    tf = t.astype(F32)
    ret_inv = ROPE_BASE ** (-jnp.linspace(0.0, 1.0, RET_QK_DIM // 2, dtype=F32))
    ret_ang = tf[:, None] * ret_inv[None, :]
    rc, rs = jnp.cos(ret_ang), jnp.sin(ret_ang)
    tok = jnp.maximum(t - N_META, 0)
    row = jnp.where(t < N_META, 0, tok // GRID_W).astype(F32)
    col = jnp.where(t < N_META, 0, tok % GRID_W).astype(F32)
    ax_half = ATT_HEAD_DIM // 2
    ax_inv = ROPE_BASE ** (-jnp.arange(ax_half // 2, dtype=F32) * 2.0 / ax_half)
    ra, ca = row[:, None] * ax_inv[None, :], col[:, None] * ax_inv[None, :]
    z = jnp.zeros_like(ra)
    return jnp.concatenate([
        rc, rc,
        -rs, rs,
        jnp.cos(ra), jnp.cos(ra), jnp.cos(ca), jnp.cos(ca),
        -jnp.sin(ra), z, -jnp.sin(ca), z,
        z, jnp.sin(ra), z, jnp.sin(ca),
    ], axis=1)


def _in_proj_kernel(h_ref, gain_ref, w_ref, tab_ref, qg_ref, kg_ref,
                    rq_ref, rk_ref, rv_ref, rg_ref, aq_ref, ak_ref, av_ref, gr_ref, ga_ref):
    x = h_ref[0]
    xg = (x * gain_ref[...]).astype(BF16)
    inv_rms = lax.rsqrt(jnp.mean(x * x, axis=-1, keepdims=True) + NORM_EPS)
    offsets = {}
    lo = 0
    for name, width in zip(IN_NAMES, IN_SPLITS):
        offsets[name] = (lo, width)
        lo += width

    def tab(i):
        return tab_ref[:, i * LANES:(i + 1) * LANES]

    def proj(name):
        lo, width = offsets[name]
        return jnp.dot(xg, w_ref[:, lo:lo + width], preferred_element_type=F32) * inv_rms

    def ret_rope(x, scale):
        y = x * tab(TAB_RC) + pltpu.roll(x, RET_QK_DIM // 2, 1) * tab(TAB_RS)
        return y if scale is None else y * scale

    def axial(x, gain, scale):
        y = _rms_rows(x) * gain
        y = (y * tab(TAB_AC) + pltpu.roll(y, LANES - 32, 1) * tab(TAB_ASP)
             + pltpu.roll(y, 32, 1) * tab(TAB_ASM))
        return y if scale is None else y * scale

    def heads(x, out_ref, fn):
        for hd in range(x.shape[1] // LANES):
            sl = slice(hd * LANES, (hd + 1) * LANES)
            out_ref[0, :, sl] = fn(x[:, sl]).astype(out_ref.dtype)

    heads(proj("aq"), aq_ref, lambda y: axial(y, qg_ref[...], ATT_HEAD_DIM ** -0.5))
    heads(proj("ak"), ak_ref, lambda y: axial(y, kg_ref[...], None))
    heads(proj("rq"), rq_ref, lambda y: ret_rope(y, None))
    heads(proj("rk"), rk_ref, lambda y: ret_rope(y, RET_QK_DIM ** -0.5))
    for name, out_ref in (("av", av_ref), ("rv", rv_ref), ("rg", rg_ref), ("gr", gr_ref),
                          ("ga", ga_ref)):
        out_ref[0] = proj(name).astype(out_ref.dtype)


def _in_proj(h, gain, w_in, tabs, q_gain, k_gain):
    b, lp, _ = h.shape
    tm = _row_tile(lp)
    widths = IN_SPLITS

    def rows(w):
        return pl.BlockSpec((1, tm, w), lambda i, j: (i, j, 0))

    return pl.pallas_call(
        _in_proj_kernel,
        grid=(b, lp // tm),
        in_specs=[rows(D_MODEL), _resident((1, D_MODEL)), _resident((D_MODEL, IN_WIDTH)),
                  pl.BlockSpec((tm, TAB_W), lambda i, j: (j, 0)),
                  _resident((1, LANES)), _resident((1, LANES))],
        out_specs=[rows(w) for w in widths],
        out_shape=[jax.ShapeDtypeStruct((b, lp, w), BF16) for w in widths],
        compiler_params=_params(2),
        name="in_proj",
    )(h, gain, w_in, tabs, q_gain, k_gain)


COEF_DECAY, COEF_Q_FWD, COEF_Q_BWD, COEF_K_FWD, COEF_K_BWD = range(5)
FWD, BWD = 0, 1


def _retention_kernel(lg_ref, q_ref, k_ref, v_ref, g_ref, o_ref, coef_ref, state_ref, stack_ref):
    nc = q_ref.shape[1] // BLOCK
    row = lax.broadcasted_iota(jnp.int32, (BLOCK, BLOCK), 0).astype(F32)
    col = lax.broadcasted_iota(jnp.int32, (BLOCK, BLOCK), 1).astype(F32)
    rel = row - col
    chunk_decay = []
    for hd in range(RET_HEADS):
        log_f, log_b = lg_ref[FWD, hd], lg_ref[BWD, hd]
        coef_ref[hd, COEF_DECAY] = jnp.where(rel >= 0, jnp.exp(log_f * jnp.maximum(rel, 0.0)),
                                             jnp.exp(log_b * jnp.maximum(-rel, 0.0)))
        coef_ref[hd, COEF_Q_FWD] = jnp.exp(log_f * (row + 1.0))
        coef_ref[hd, COEF_Q_BWD] = jnp.exp(log_b * (BLOCK - row))
        coef_ref[hd, COEF_K_FWD] = jnp.exp(log_f * (BLOCK - 1.0 - row))
        coef_ref[hd, COEF_K_BWD] = jnp.exp(log_b * row)
        ones_row = jnp.ones((1, RET_V_DIM), F32)
        chunk_decay.append((jnp.exp(log_f * BLOCK * ones_row), jnp.exp(log_b * BLOCK * ones_row)))
    state_ref[...] = jnp.zeros_like(state_ref)

    def rows_of(c):
        return pl.ds(pl.multiple_of(c * BLOCK, BLOCK), BLOCK)

    def qk_cols(hd):
        return slice(hd * RET_QK_DIM, (hd + 1) * RET_QK_DIM)

    def v_cols(hd):
        return slice(hd * RET_V_DIM, (hd + 1) * RET_V_DIM)

    def stack_rows(c, hd, direction):
        base = (c * RET_HEADS + hd) * 2 * BLOCK + direction * BLOCK
        return pl.ds(pl.multiple_of(base, BLOCK), BLOCK)

    def scan_step(c, direction):
        r = rows_of(c)
        for hd in range(RET_HEADS):
            slot = direction * RET_HEADS + hd
            state = state_ref[slot]
            stack_ref[stack_rows(c, hd, direction), :] = state.astype(stack_ref.dtype)
            k = k_ref[0, r, qk_cols(hd)].astype(F32) * coef_ref[hd, COEF_K_FWD + direction]
            update = lax.dot_general(k.astype(BF16), v_ref[0, r, v_cols(hd)],
                                     (((0,), (0,)), ((), ())), preferred_element_type=F32)
            state_ref[slot] = state * chunk_decay[hd][direction] + update

    def scan(t, carry):
        scan_step(t, FWD)
        scan_step(nc - 1 - t, BWD)
        return carry

    def emit(c, carry):
        r = rows_of(c)
        for hd in range(RET_HEADS):
            q, k, v = q_ref[0, r, qk_cols(hd)], k_ref[0, r, qk_cols(hd)], v_ref[0, r, v_cols(hd)]
            s = lax.dot_general(q, k, (((1,), (1,)), ((), ())), preferred_element_type=F32)
            qf = q.astype(F32)
            lhs = jnp.concatenate([(s * coef_ref[hd, COEF_DECAY]).astype(BF16),
                                   (qf * coef_ref[hd, COEF_Q_FWD]).astype(BF16),
                                   (qf * coef_ref[hd, COEF_Q_BWD]).astype(BF16)], axis=1)
            states = stack_ref[pl.ds(pl.multiple_of((c * RET_HEADS + hd) * 2 * BLOCK, BLOCK),
                                     2 * BLOCK), :]
            rhs = jnp.concatenate([v, states], axis=0)
            o = _rms_rows(jnp.dot(lhs, rhs, preferred_element_type=F32))
            g = g_ref[0, r, v_cols(hd)].astype(F32)
            o_ref[0, r, v_cols(hd)] = (o * (g * _sigmoid(g))).astype(o_ref.dtype)
        return carry

    lax.fori_loop(0, nc, scan, 0)
    lax.fori_loop(0, nc, emit, 0)


def _retention(log_gamma, rq, rk, rv, rg):
    b, lp, _ = rq.shape
    nc = lp // BLOCK

    def seq(w):
        return pl.BlockSpec((1, lp, w), lambda i, lg: (i, 0, 0))

    return pl.pallas_call(
        _retention_kernel,
        grid_spec=pltpu.PrefetchScalarGridSpec(
            num_scalar_prefetch=1,
            grid=(b,),
            in_specs=[seq(RET_QK_W), seq(RET_QK_W), seq(RET_V_W), seq(RET_V_W)],
            out_specs=seq(RET_V_W),
            scratch_shapes=[pltpu.VMEM((RET_HEADS, 5, BLOCK, BLOCK), F32),
                            pltpu.VMEM((2 * RET_HEADS, RET_QK_DIM, RET_V_DIM), F32),
                            pltpu.VMEM((nc * RET_HEADS * 2 * BLOCK, RET_V_DIM), BF16)],
        ),
        out_shape=jax.ShapeDtypeStruct((b, lp, RET_V_W), BF16),
        compiler_params=_params(1),
        name="retention",
    )(log_gamma, rq, rk, rv, rg)


def _attention_kernel(q_ref, k_ref, v_ref, o_ref):
    tq = q_ref.shape[1]
    lp = k_ref.shape[1]
    sub = _row_tile(tq, ATT_SUB_ROWS)
    key = lax.broadcasted_iota(jnp.int32, (1, lp), 1)
    for r0 in range(0, tq, sub):
        for g in range(ATT_GROUP):
            rows, cols = slice(r0, r0 + sub), slice(g * LANES, (g + 1) * LANES)
            s = lax.dot_general(q_ref[0, rows, cols], k_ref[0], (((1,), (1,)), ((), ())),
                                preferred_element_type=F32)
            s = jnp.where(key >= PAD, s, MASK_VALUE)
            p = jnp.exp(s - jnp.max(s, axis=-1, keepdims=True))
            denom = jnp.sum(p, axis=-1, keepdims=True)
            o = jnp.dot(p.astype(BF16), v_ref[0], preferred_element_type=F32) * (1.0 / denom)
            o_ref[0, rows, cols] = o.astype(o_ref.dtype)


def _attention(aq, ak, av):
    b, lp, _ = aq.shape
    tq = _row_tile(lp, ATT_Q_ROWS)
    return pl.pallas_call(
        _attention_kernel,
        grid=(b, ATT_KV_HEADS, lp // tq),
        in_specs=[pl.BlockSpec((1, tq, ATT_GROUP_W), lambda i, j, t: (i, t, j)),
                  pl.BlockSpec((1, lp, ATT_HEAD_DIM), lambda i, j, t: (i, 0, j)),
                  pl.BlockSpec((1, lp, ATT_HEAD_DIM), lambda i, j, t: (i, 0, j))],
        out_specs=pl.BlockSpec((1, tq, ATT_GROUP_W), lambda i, j, t: (i, t, j)),
        out_shape=jax.ShapeDtypeStruct((b, lp, ATT_Q_W), BF16),
        compiler_params=_params(3),
        name="attention",
    )(aq, ak, av)


def _mix_out_kernel(h_ref, ret_ref, att_ref, gr_ref, ga_ref, wr_ref, wa_ref, wo_ref, o_ref):
    tm = h_ref.shape[1]
    r = jnp.dot(ret_ref[0], wr_ref[...], preferred_element_type=F32)
    a = jnp.dot(att_ref[0], wa_ref[...], preferred_element_type=F32)
    merged = _sigmoid(gr_ref[0].astype(F32)) * r + _sigmoid(ga_ref[0].astype(F32)) * a
    out = h_ref[0] + jnp.dot(merged.astype(BF16), wo_ref[...], preferred_element_type=F32)
    row = pl.program_id(1) * tm + lax.broadcasted_iota(jnp.int32, (tm, 1), 0)
    o_ref[0] = jnp.where(row >= PAD, out, 0.0)


def _mix_out(h, ret, att, gr, ga, w_ret_o, w_att_o, w_out):
    b, lp, _ = h.shape
    tm = _row_tile(lp)
    rows = pl.BlockSpec((1, tm, D_MODEL), lambda i, j: (i, j, 0))
    w = _resident((D_MODEL, D_MODEL))
    return pl.pallas_call(
        _mix_out_kernel,
        grid=(b, lp // tm),
        in_specs=[rows] * 5 + [w] * 3,
        out_specs=rows,
        out_shape=jax.ShapeDtypeStruct(h.shape, F32),
        input_output_aliases={0: 0},
        compiler_params=_params(2),
        name="mix_out",
    )(h, ret, att, gr, ga, w_ret_o, w_att_o, w_out)


def _ffn_kernel(h_ref, gain_ref, wi_ref, wo_ref, o_ref):
    x = h_ref[0]
    xg = (x * gain_ref[...]).astype(BF16)
    inv_rms = lax.rsqrt(jnp.mean(x * x, axis=-1, keepdims=True) + NORM_EPS)
    a = jnp.dot(xg, wi_ref[:, :D_FF], preferred_element_type=F32) * inv_rms
    u = jnp.dot(xg, wi_ref[:, D_FF:], preferred_element_type=F32) * inv_rms
    act = (a * _sigmoid(a) * u).astype(BF16)
    o_ref[0] = x + jnp.dot(act, wo_ref[...], preferred_element_type=F32)


def _ffn(h, gain, w_ffn_in, w_ffn_out):
    b, lp, _ = h.shape
    tm = _row_tile(lp)
    rows = pl.BlockSpec((1, tm, D_MODEL), lambda i, j: (i, j, 0))
    return pl.pallas_call(
        _ffn_kernel,
        grid=(b, lp // tm),
        in_specs=[rows, _resident((1, D_MODEL)), _resident((D_MODEL, 2 * D_FF)),
                  _resident((D_FF, D_MODEL))],
        out_specs=rows,
        out_shape=jax.ShapeDtypeStruct(h.shape, F32),
        input_output_aliases={0: 0},
        compiler_params=_params(2),
        name="ffn",
    )(h, gain, w_ffn_in, w_ffn_out)


def _final_norm_kernel(h_ref, gain_ref, o_ref):
    o_ref[0] = _rms_rows(h_ref[0, BLOCK:, :]) * gain_ref[...]


def _final_norm(h, gain, first, count):
    _, lp, _ = h.shape
    s = lp - BLOCK
    return pl.pallas_call(
        _final_norm_kernel,
        grid=(count,),
        in_specs=[pl.BlockSpec((1, lp, D_MODEL), lambda i: (i + first, 0, 0)),
                  _resident((1, D_MODEL))],
        out_specs=pl.BlockSpec((1, s, D_MODEL), lambda i: (i, 0, 0)),
        out_shape=jax.ShapeDtypeStruct((count, s, D_MODEL), F32),
        compiler_params=_params(1),
        name="final_norm",
    )(h, gain)


def _trunk(xs, meta_tokens, norm_mix, w_in, ret_decay, q_norm, k_norm, w_ret_o, w_att_o, w_out,
           norm_ffn, w_ffn_in, w_ffn_out, norm_final):
    s = xs[0].shape[1]
    assert all(x.shape[1:] == (s, D_MODEL) for x in xs) and s % GRID_W == 0
    lp = BLOCK + s
    counts = [x.shape[0] for x in xs]
    b = sum(counts)
    head = jnp.concatenate([jnp.zeros((PAD, D_MODEL), F32), meta_tokens.astype(F32)], axis=0)
    h = jnp.concatenate([jnp.broadcast_to(head[None], (b, BLOCK, D_MODEL)),
                         jnp.concatenate(xs, axis=0)], axis=1)
    tabs = _rotary_tables(lp)
    gamma = 1.0 - jnp.exp2(-ret_decay.astype(F32))
    log_gamma = jnp.log(gamma)
    for l in range(DEPTH):
        rq, rk, rv, rg, aq, ak, av, gr, ga = _in_proj(
            h, norm_mix[l][None], w_in[l].astype(BF16), tabs, q_norm[l][None], k_norm[l][None])
        ret = _retention(log_gamma[l], rq, rk, rv, rg)
        att = _attention(aq, ak, av)
        h = _mix_out(h, ret, att, gr, ga, w_ret_o[l].astype(BF16), w_att_o[l].astype(BF16),
                     w_out[l].astype(BF16))
        h = _ffn(h, norm_ffn[l][None], w_ffn_in[l].astype(BF16), w_ffn_out[l].astype(BF16))
    outs, first = [], 0
    for c in counts:
        outs.append(_final_norm(h, norm_final[None], first, c))
        first += c
    return outs


def kernel(x_prompt, x_sample, meta_tokens, norm_mix, w_in, ret_decay, q_norm, k_norm, w_ret_o,
           w_att_o, w_out, norm_ffn, w_ffn_in, w_ffn_out, norm_final):
    y_prompt, y_sample = _trunk([x_prompt, x_sample], meta_tokens, norm_mix, w_in, ret_decay,
                                q_norm, k_norm, w_ret_o, w_att_o, w_out, norm_ffn, w_ffn_in,
                                w_ffn_out, norm_final)
    return (y_prompt, y_sample)
```

```python
import functools

import jax
import jax.numpy as jnp
from jax import lax
from jax.experimental import pallas as pl
from jax.experimental.pallas import tpu as pltpu

D_MODEL = 1024
DEPTH = 2
N_META = 16
GRID_W = 64
BLOCK = 128
PAD = BLOCK - N_META
RET_HEADS = 4
RET_QK_DIM = 128
RET_V_DIM = 256
ATT_HEADS = 8
ATT_KV_HEADS = 2
ATT_GROUP = ATT_HEADS // ATT_KV_HEADS
ATT_HEAD_DIM = 128
D_FF = ((8 * D_MODEL + 3 * 256 - 1) // (3 * 256)) * 256
ROPE_BASE = 10000.0
NORM_EPS = 1e-6
RET_QK_W = RET_HEADS * RET_QK_DIM
RET_V_W = RET_HEADS * RET_V_DIM
ATT_Q_W = ATT_HEADS * ATT_HEAD_DIM
ATT_KV_W = ATT_KV_HEADS * ATT_HEAD_DIM
ATT_GROUP_W = ATT_GROUP * ATT_HEAD_DIM
IN_NAMES = ("rq", "rk", "rv", "rg", "aq", "ak", "av", "gr", "ga")
IN_SPLITS = (RET_QK_W, RET_QK_W, RET_V_W, RET_V_W, ATT_Q_W, ATT_KV_W, ATT_KV_W, D_MODEL, D_MODEL)
IN_WIDTH = sum(IN_SPLITS)

LANES = 128
BF16_SUBLANES = 16
VMEM_BYTES_V7X = 64 * 1024 * 1024
VMEM_LIMIT = VMEM_BYTES_V7X * 7 // 8
ROW_TILE_TARGET = 688
ATT_Q_ROWS = 1088
ATT_SUB_ROWS = 272
MASK_VALUE = -1e30

F32 = jnp.float32
BF16 = jnp.bfloat16

TAB_RC, TAB_RS, TAB_AC, TAB_ASP, TAB_ASM = range(5)
TAB_W = 5 * LANES


def _row_tile(rows, target=None):
    target = ROW_TILE_TARGET if target is None else target
    best = None
    for d in range(BF16_SUBLANES, min(rows, target) + 1, BF16_SUBLANES):
        if rows % d == 0:
            best = d
    assert best is not None, rows
    return best


def _params(n_axes):
    return pltpu.CompilerParams(dimension_semantics=("arbitrary",) * n_axes,
                                vmem_limit_bytes=VMEM_LIMIT)


def _resident(shape):
    return pl.BlockSpec(shape, lambda *_: (0,) * len(shape), pipeline_mode=pl.Buffered(1))


def _rms_rows(x):
    return x * lax.rsqrt(jnp.mean(x * x, axis=-1, keepdims=True) + NORM_EPS)


def _sigmoid(x):
    return 1.0 / (1.0 + jnp.exp(-x))


def _rotary_tables(s):
    r = jnp.arange(s + N_META, dtype=jnp.int32)
    is_tok = r < s
    t = jnp.where(is_tok, r + N_META, r - s)
    ret_inv = ROPE_BASE ** (-jnp.linspace(0.0, 1.0, RET_QK_DIM // 2, dtype=F32))
    ret_ang = t.astype(F32)[:, None] * ret_inv[None, :]
    rc, rs = jnp.cos(ret_ang), jnp.sin(ret_ang)
    row = jnp.where(is_tok, r // GRID_W, 0).astype(F32)
    col = jnp.where(is_tok, r % GRID_W, 0).astype(F32)
    ax_half = ATT_HEAD_DIM // 2
    ax_inv = ROPE_BASE ** (-jnp.arange(ax_half // 2, dtype=F32) * 2.0 / ax_half)
    ra, ca = row[:, None] * ax_inv[None, :], col[:, None] * ax_inv[None, :]
    z = jnp.zeros_like(ra)
    return jnp.concatenate([
        rc, rc,
        -rs, rs,
        jnp.cos(ra), jnp.cos(ra), jnp.cos(ca), jnp.cos(ca),
        -jnp.sin(ra), z, -jnp.sin(ca), z,
        z, jnp.sin(ra), z, jnp.sin(ca),
    ], axis=1)


class _Rows:
    def __init__(self, tm, nt, s, counts, first_layer):
        self.tm, self.nt, self.s, self.counts, self.first_layer = tm, nt, s, counts, first_layer

    def specs(self):
        tm, nt = self.tm, self.nt
        if not self.first_layer:
            return [pl.BlockSpec((1, tm, D_MODEL), lambda i, j: (i, jnp.minimum(j, nt - 1), 0))]
        n0 = self.counts[0]

        def first(i, j):
            mine = i < n0
            return (jnp.where(mine, i, n0 - 1), jnp.where(mine, jnp.minimum(j, nt - 1), nt - 1), 0)

        def second(i, j):
            mine = i >= n0
            return (jnp.where(mine, i - n0, 0), jnp.where(mine, jnp.minimum(j, nt - 1), 0), 0)

        return [pl.BlockSpec((1, tm, D_MODEL), first), pl.BlockSpec((1, tm, D_MODEL), second),
                _resident((N_META, D_MODEL))]

    def n_refs(self):
        return 3 if self.first_layer else 1

    def load(self, refs):
        if not self.first_layer:
            return refs[0][0]
        xa_ref, xb_ref, meta_ref = refs
        i, j = pl.program_id(0), pl.program_id(1)
        x = jnp.where(i < self.counts[0], xa_ref[0], xb_ref[0])
        meta_at = self.s - (self.nt - 1) * self.tm
        assert meta_at + N_META == self.tm
        meta_tile = jnp.concatenate([jnp.zeros((meta_at, D_MODEL), F32), meta_ref[...]], axis=0)
        row = j * self.tm + lax.broadcasted_iota(jnp.int32, (self.tm, 1), 0)
        return jnp.where(row < self.s, x, meta_tile)


def _row_inputs(rows, h, xs, meta):
    return [xs[0], xs[1], meta] if rows.first_layer else [h]


def _in_proj_kernel(rows, *refs):
    row_refs, refs = refs[:rows.n_refs()], refs[rows.n_refs():]
    gain_ref, w_ref, tab_ref, qg_ref, kg_ref = refs[:5]
    outs = dict(zip(IN_NAMES, refs[5:]))
    offsets = {}
    lo = 0
    for name, width in zip(IN_NAMES, IN_SPLITS):
        offsets[name] = (lo, width)
        lo += width

    def tab(i):
        return tab_ref[:, i * LANES:(i + 1) * LANES]

    def ret_rope(x, scale):
        y = x * tab(TAB_RC) + pltpu.roll(x, RET_QK_DIM // 2, 1) * tab(TAB_RS)
        return y if scale is None else y * scale

    def axial(x, gain, scale):
        y = _rms_rows(x) * gain
        y = (y * tab(TAB_AC) + pltpu.roll(y, LANES - 32, 1) * tab(TAB_ASP)
             + pltpu.roll(y, 32, 1) * tab(TAB_ASM))
        return y if scale is None else y * scale

    @pl.when(pl.program_id(1) < rows.nt)
    def _():
        x = rows.load(row_refs)
        xg = (x * gain_ref[...]).astype(BF16)
        inv_rms = lax.rsqrt(jnp.mean(x * x, axis=-1, keepdims=True) + NORM_EPS)

        def proj(name):
            lo, width = offsets[name]
            return jnp.dot(xg, w_ref[:, lo:lo + width], preferred_element_type=F32) * inv_rms

        def heads(name, fn):
            y = proj(name)
            for hd in range(y.shape[1] // LANES):
                sl = slice(hd * LANES, (hd + 1) * LANES)
                outs[name][0, :, sl] = fn(y[:, sl]).astype(outs[name].dtype)

        heads("aq", lambda y: axial(y, qg_ref[...], ATT_HEAD_DIM ** -0.5))
        heads("ak", lambda y: axial(y, kg_ref[...], None))
        heads("rq", lambda y: ret_rope(y, None))
        heads("rk", lambda y: ret_rope(y, RET_QK_DIM ** -0.5))
        for name in ("av", "rv", "rg", "gr", "ga"):
            outs[name][0] = proj(name).astype(outs[name].dtype)

    @pl.when(pl.program_id(1) == rows.nt)
    def _():
        for name in IN_NAMES:
            outs[name][...] = jnp.zeros_like(outs[name])


def _in_proj(rows, h, xs, meta, gain, w_in, tabs, q_gain, k_gain, b, lp):
    tm, nt = rows.tm, rows.nt
    assert lp - nt * tm <= tm

    def out_rows(w):
        return pl.BlockSpec((1, tm, w), lambda i, j: (i, j, 0))

    return pl.pallas_call(
        functools.partial(_in_proj_kernel, rows),
        grid=(b, nt + 1),
        in_specs=rows.specs() + [_resident((1, D_MODEL)), _resident((D_MODEL, IN_WIDTH)),
                                 pl.BlockSpec((tm, TAB_W), lambda i, j: (jnp.minimum(j, nt - 1), 0)),
                                 _resident((1, LANES)), _resident((1, LANES))],
        out_specs=[out_rows(w) for w in IN_SPLITS],
        out_shape=[jax.ShapeDtypeStruct((b, lp, w), BF16) for w in IN_SPLITS],
        compiler_params=_params(2),
        name="in_proj",
    )(*_row_inputs(rows, h, xs, meta), gain, w_in, tabs, q_gain, k_gain)


(COEF_DECAY, COEF_Q_FWD, COEF_Q_BWD, COEF_K_FWD, COEF_K_BWD,
 COEF_Q_FWD_META, COEF_Q_BWD_META, COEF_K_FWD_META, COEF_K_BWD_META) = range(9)
N_COEF = 9
FWD, BWD = 0, 1


def _retention_kernel(lg_ref, q_ref, k_ref, v_ref, g_ref, o_ref, coef_ref, state_ref, stack_ref):
    nc = q_ref.shape[1] // BLOCK
    row = lax.broadcasted_iota(jnp.int32, (BLOCK, BLOCK), 0).astype(F32)
    col = lax.broadcasted_iota(jnp.int32, (BLOCK, BLOCK), 1).astype(F32)
    rel = row - col
    meta_pos = jnp.minimum(row + PAD, BLOCK - 1.0)
    chunk_decay = []
    for hd in range(RET_HEADS):
        log_f, log_b = lg_ref[FWD, hd], lg_ref[BWD, hd]
        coef_ref[hd, COEF_DECAY] = jnp.where(rel >= 0, jnp.exp(log_f * jnp.maximum(rel, 0.0)),
                                             jnp.exp(log_b * jnp.maximum(-rel, 0.0)))
        for pos, shift in ((row, 0), (meta_pos, COEF_Q_FWD_META - COEF_Q_FWD)):
            coef_ref[hd, COEF_Q_FWD + shift] = jnp.exp(log_f * (pos + 1.0))
            coef_ref[hd, COEF_Q_BWD + shift] = jnp.exp(log_b * (BLOCK - pos))
            coef_ref[hd, COEF_K_FWD + shift] = jnp.exp(log_f * (BLOCK - 1.0 - pos))
            coef_ref[hd, COEF_K_BWD + shift] = jnp.exp(log_b * pos)
        ones_row = jnp.ones((1, RET_V_DIM), F32)
        chunk_decay.append((jnp.exp(log_f * BLOCK * ones_row), jnp.exp(log_b * BLOCK * ones_row)))
    state_ref[...] = jnp.zeros_like(state_ref)

    def physical(c):
        if c == 0:
            return slice((nc - 1) * BLOCK, nc * BLOCK), COEF_Q_FWD_META - COEF_Q_FWD
        return slice((c - 1) * BLOCK, c * BLOCK), 0

    def qk_cols(hd):
        return slice(hd * RET_QK_DIM, (hd + 1) * RET_QK_DIM)

    def v_cols(hd):
        return slice(hd * RET_V_DIM, (hd + 1) * RET_V_DIM)

    def stack_rows(c, hd, direction=None):
        base = (c * RET_HEADS + hd) * 2 * BLOCK
        if direction is None:
            return slice(base, base + 2 * BLOCK)
        return slice(base + direction * BLOCK, base + (direction + 1) * BLOCK)

    def scan_step(c, direction, update):
        r, shift = physical(c)
        for hd in range(RET_HEADS):
            slot = direction * RET_HEADS + hd
            state = state_ref[slot]
            stack_ref[stack_rows(c, hd, direction), :] = state.astype(stack_ref.dtype)
            if update:
                k = (k_ref[0, r, qk_cols(hd)].astype(F32)
                     * coef_ref[hd, COEF_K_FWD + direction + shift])
                pushed = lax.dot_general(k.astype(BF16), v_ref[0, r, v_cols(hd)],
                                         (((0,), (0,)), ((), ())), preferred_element_type=F32)
                state_ref[slot] = state * chunk_decay[hd][direction] + pushed

    for t in range(nc):
        scan_step(t, FWD, update=t < nc - 1)
        scan_step(nc - 1 - t, BWD, update=t < nc - 1)

    for c in range(nc):
        r, shift = physical(c)
        for hd in range(RET_HEADS):
            q, k, v = q_ref[0, r, qk_cols(hd)], k_ref[0, r, qk_cols(hd)], v_ref[0, r, v_cols(hd)]
            s = lax.dot_general(q, k, (((1,), (1,)), ((), ())), preferred_element_type=F32)
            qf = q.astype(F32)
            lhs = jnp.concatenate([(s * coef_ref[hd, COEF_DECAY]).astype(BF16),
                                   (qf * coef_ref[hd, COEF_Q_FWD + shift]).astype(BF16),
                                   (qf * coef_ref[hd, COEF_Q_BWD + shift]).astype(BF16)], axis=1)
            rhs = jnp.concatenate([v, stack_ref[stack_rows(c, hd), :]], axis=0)
            o = _rms_rows(jnp.dot(lhs, rhs, preferred_element_type=F32))
            g = g_ref[0, r, v_cols(hd)].astype(F32)
            o_ref[0, r, v_cols(hd)] = (o * (g * _sigmoid(g))).astype(o_ref.dtype)


def _retention(log_gamma, rq, rk, rv, rg):
    b, lp, _ = rq.shape
    nc = lp // BLOCK

    def seq(w):
        return pl.BlockSpec((1, lp, w), lambda i, lg: (i, 0, 0))

    return pl.pallas_call(
        _retention_kernel,
        grid_spec=pltpu.PrefetchScalarGridSpec(
            num_scalar_prefetch=1,
            grid=(b,),
            in_specs=[seq(RET_QK_W), seq(RET_QK_W), seq(RET_V_W), seq(RET_V_W)],
            out_specs=seq(RET_V_W),
            scratch_shapes=[pltpu.VMEM((RET_HEADS, N_COEF, BLOCK, BLOCK), F32),
                            pltpu.VMEM((2 * RET_HEADS, RET_QK_DIM, RET_V_DIM), F32),
                            pltpu.VMEM((nc * RET_HEADS * 2 * BLOCK, RET_V_DIM), BF16)],
        ),
        out_shape=jax.ShapeDtypeStruct((b, lp, RET_V_W), BF16),
        compiler_params=_params(1),
        name="retention",
    )(log_gamma, rq, rk, rv, rg)


def _attention_kernel(q_ref, k_ref, v_ref, o_ref):
    tq = q_ref.shape[1]
    lp = k_ref.shape[1]
    sub = _row_tile(tq, ATT_SUB_ROWS)
    key = lax.broadcasted_iota(jnp.int32, (1, lp), 1)
    v1 = jnp.concatenate([v_ref[0], jnp.ones((lp, LANES), v_ref.dtype)], axis=1)
    for r0 in range(0, tq, sub):
        for g in range(ATT_GROUP):
            rows, cols = slice(r0, r0 + sub), slice(g * LANES, (g + 1) * LANES)
            s = lax.dot_general(q_ref[0, rows, cols], k_ref[0], (((1,), (1,)), ((), ())),
                                preferred_element_type=F32)
            s = jnp.where(key < lp - PAD, s, MASK_VALUE)
            p = jnp.exp(s - jnp.max(s, axis=-1, keepdims=True))
            ov = jnp.dot(p.astype(BF16), v1, preferred_element_type=F32)
            o = ov[:, :ATT_HEAD_DIM] * (1.0 / ov[:, ATT_HEAD_DIM:])
            o_ref[0, rows, cols] = o.astype(o_ref.dtype)


def _attention(aq, ak, av):
    b, lp, _ = aq.shape
    tq = _row_tile(lp, ATT_Q_ROWS)
    return pl.pallas_call(
        _attention_kernel,
        grid=(b, ATT_KV_HEADS, lp // tq),
        in_specs=[pl.BlockSpec((1, tq, ATT_GROUP_W), lambda i, j, t: (i, t, j)),
                  pl.BlockSpec((1, lp, ATT_HEAD_DIM), lambda i, j, t: (i, 0, j)),
                  pl.BlockSpec((1, lp, ATT_HEAD_DIM), lambda i, j, t: (i, 0, j))],
        out_specs=pl.BlockSpec((1, tq, ATT_GROUP_W), lambda i, j, t: (i, t, j)),
        out_shape=jax.ShapeDtypeStruct((b, lp, ATT_Q_W), BF16),
        compiler_params=_params(3),
        name="attention",
    )(aq, ak, av)


def _mix_out_kernel(rows, *refs):
    row_refs, refs = refs[:rows.n_refs()], refs[rows.n_refs():]
    ret_ref, att_ref, gr_ref, ga_ref, wr_ref, wa_ref, wo_ref, o_ref = refs
    r = jnp.dot(ret_ref[0], wr_ref[...], preferred_element_type=F32)
    a = jnp.dot(att_ref[0], wa_ref[...], preferred_element_type=F32)
    merged = _sigmoid(gr_ref[0].astype(F32)) * r + _sigmoid(ga_ref[0].astype(F32)) * a
    o_ref[0] = rows.load(row_refs) + jnp.dot(merged.astype(BF16), wo_ref[...],
                                             preferred_element_type=F32)


def _mix_out(rows, h, xs, meta, ret, att, gr, ga, w_ret_o, w_att_o, w_out, b):
    tm, nt = rows.tm, rows.nt
    tile = pl.BlockSpec((1, tm, D_MODEL), lambda i, j: (i, j, 0))
    w = _resident((D_MODEL, D_MODEL))
    return pl.pallas_call(
        functools.partial(_mix_out_kernel, rows),
        grid=(b, nt),
        in_specs=rows.specs() + [tile] * 4 + [w] * 3,
        out_specs=tile,
        out_shape=jax.ShapeDtypeStruct((b, nt * tm, D_MODEL), F32),
        input_output_aliases={} if rows.first_layer else {0: 0},
        compiler_params=_params(2),
        name="mix_out",
    )(*_row_inputs(rows, h, xs, meta), ret, att, gr, ga, w_ret_o, w_att_o, w_out)


def _ffn_kernel(final_counts, h_ref, gain_ref, wi_ref, wo_ref, *rest):
    x = h_ref[0]
    xg = (x * gain_ref[...]).astype(BF16)
    inv_rms = lax.rsqrt(jnp.mean(x * x, axis=-1, keepdims=True) + NORM_EPS)
    a = jnp.dot(xg, wi_ref[:, :D_FF], preferred_element_type=F32) * inv_rms
    u = jnp.dot(xg, wi_ref[:, D_FF:], preferred_element_type=F32) * inv_rms
    act = (a * _sigmoid(a) * u).astype(BF16)
    out = x + jnp.dot(act, wo_ref[...], preferred_element_type=F32)
    if final_counts is None:
        (o_ref,) = rest
        o_ref[0] = out
        return
    fgain_ref, ya_ref, yb_ref = rest
    y = _rms_rows(out) * fgain_ref[...]

    @pl.when(pl.program_id(0) < final_counts[0])
    def _():
        ya_ref[0] = y

    @pl.when(pl.program_id(0) >= final_counts[0])
    def _():
        yb_ref[0] = y


def _ffn(h, gain, w_ffn_in, w_ffn_out, tm, final=None):
    b, lh, _ = h.shape
    nt = lh // tm
    tile = pl.BlockSpec((1, tm, D_MODEL), lambda i, j: (i, j, 0))
    in_specs = [tile, _resident((1, D_MODEL)), _resident((D_MODEL, 2 * D_FF)),
                _resident((D_FF, D_MODEL))]
    if final is None:
        return pl.pallas_call(
            functools.partial(_ffn_kernel, None),
            grid=(b, nt),
            in_specs=in_specs,
            out_specs=tile,
            out_shape=jax.ShapeDtypeStruct(h.shape, F32),
            input_output_aliases={0: 0},
            compiler_params=_params(2),
            name="ffn",
        )(h, gain, w_ffn_in, w_ffn_out)
    fgain, counts, s = final
    n0 = counts[0]

    def first(i, j):
        mine = i < n0
        return (jnp.where(mine, i, n0 - 1), jnp.where(mine, j, nt - 1), 0)

    def second(i, j):
        mine = i >= n0
        return (jnp.where(mine, i - n0, 0), jnp.where(mine, j, 0), 0)

    return pl.pallas_call(
        functools.partial(_ffn_kernel, counts),
        grid=(b, nt),
        in_specs=in_specs + [_resident((1, D_MODEL))],
        out_specs=[pl.BlockSpec((1, tm, D_MODEL), first), pl.BlockSpec((1, tm, D_MODEL), second)],
        out_shape=[jax.ShapeDtypeStruct((c, s, D_MODEL), F32) for c in counts],
        compiler_params=_params(2),
        name="ffn_final",
    )(h, gain, w_ffn_in, w_ffn_out, fgain)


def _trunk(xs, meta_tokens, norm_mix, w_in, ret_decay, q_norm, k_norm, w_ret_o, w_att_o, w_out,
           norm_ffn, w_ffn_in, w_ffn_out, norm_final):
    assert len(xs) == 2
    s = xs[0].shape[1]
    assert all(x.shape[1:] == (s, D_MODEL) for x in xs) and s % GRID_W == 0 and s % BLOCK == 0
    lh, lp = s + N_META, s + BLOCK
    counts = tuple(x.shape[0] for x in xs)
    b = sum(counts)
    tm = _row_tile(lh)
    nt = lh // tm
    assert s > (nt - 1) * tm, "the meta tokens must sit in the last row tile"
    meta = meta_tokens.astype(F32)
    tabs = _rotary_tables(s)
    gamma = 1.0 - jnp.exp2(-ret_decay.astype(F32))
    log_gamma = jnp.log(gamma)
    h = None
    for l in range(DEPTH):
        rows = _Rows(tm, nt, s, counts, first_layer=l == 0)
        rq, rk, rv, rg, aq, ak, av, gr, ga = _in_proj(
            rows, h, xs, meta, norm_mix[l][None], w_in[l].astype(BF16), tabs, q_norm[l][None],
            k_norm[l][None], b, lp)
        ret = _retention(log_gamma[l], rq, rk, rv, rg)
        att = _attention(aq, ak, av)
        h = _mix_out(rows, h, xs, meta, ret, att, gr, ga, w_ret_o[l].astype(BF16),
                     w_att_o[l].astype(BF16), w_out[l].astype(BF16), b)
        final = (norm_final[None], counts, s) if l == DEPTH - 1 else None
        h = _ffn(h, norm_ffn[l][None], w_ffn_in[l].astype(BF16), w_ffn_out[l].astype(BF16), tm,
                 final)
    return h


def kernel(x_prompt, x_sample, meta_tokens, norm_mix, w_in, ret_decay, q_norm, k_norm, w_ret_o,
           w_att_o, w_out, norm_ffn, w_ffn_in, w_ffn_out, norm_final):
    y_prompt, y_sample = _trunk([x_prompt, x_sample], meta_tokens, norm_mix, w_in, ret_decay,
                                q_norm, k_norm, w_ret_o, w_att_o, w_out, norm_ffn, w_ffn_in,
                                w_ffn_out, norm_final)
    return (y_prompt, y_sample)
```

```python
import functools

import jax
import jax.numpy as jnp
from jax import lax
from jax.experimental import pallas as pl
from jax.experimental.pallas import tpu as pltpu

D_MODEL = 1024
DEPTH = 2
N_META = 16
GRID_W = 64
BLOCK = 128
PAD = BLOCK - N_META
RET_HEADS = 4
RET_QK_DIM = 128
RET_V_DIM = 256
ATT_HEADS = 8
ATT_KV_HEADS = 2
ATT_GROUP = ATT_HEADS // ATT_KV_HEADS
ATT_HEAD_DIM = 128
D_FF = ((8 * D_MODEL + 3 * 256 - 1) // (3 * 256)) * 256
ROPE_BASE = 10000.0
NORM_EPS = 1e-6
RET_QK_W = RET_HEADS * RET_QK_DIM
RET_V_W = RET_HEADS * RET_V_DIM
ATT_Q_W = ATT_HEADS * ATT_HEAD_DIM
ATT_KV_W = ATT_KV_HEADS * ATT_HEAD_DIM
ATT_GROUP_W = ATT_GROUP * ATT_HEAD_DIM
IN_NAMES = ("rq", "rk", "rv", "rg", "aq", "ak", "av", "gr", "ga")
IN_SPLITS = (RET_QK_W, RET_QK_W, RET_V_W, RET_V_W, ATT_Q_W, ATT_KV_W, ATT_KV_W, D_MODEL, D_MODEL)
IN_WIDTH = sum(IN_SPLITS)

LANES = 128
BF16_SUBLANES = 16
VMEM_BYTES_V7X = 64 * 1024 * 1024
VMEM_LIMIT = VMEM_BYTES_V7X * 7 // 8
ROW_TILE_TARGET = 688
ATT_Q_ROWS = 1088
ATT_SUB_ROWS = 272
MASK_VALUE = -1e30

F32 = jnp.float32
BF16 = jnp.bfloat16

TAB_RC, TAB_RS, TAB_AC, TAB_ASP, TAB_ASM = range(5)
TAB_W = 5 * LANES


def _row_tile(rows, target=None):
    target = ROW_TILE_TARGET if target is None else target
    best = None
    for d in range(BF16_SUBLANES, min(rows, target) + 1, BF16_SUBLANES):
        if rows % d == 0:
            best = d
    assert best is not None, rows
    return best


def _params(n_axes):
    return pltpu.CompilerParams(dimension_semantics=("arbitrary",) * n_axes,
                                vmem_limit_bytes=VMEM_LIMIT)


def _resident(shape):
    return pl.BlockSpec(shape, lambda *_: (0,) * len(shape), pipeline_mode=pl.Buffered(1))


def _rms_rows(x):
    return x * lax.rsqrt(jnp.mean(x * x, axis=-1, keepdims=True) + NORM_EPS)


def _sigmoid(x):
    return 1.0 / (1.0 + jnp.exp(-x))


def _rotary_tables(s):
    r = jnp.arange(s + N_META, dtype=jnp.int32)
    is_tok = r < s
    t = jnp.where(is_tok, r + N_META, r - s)
    ret_inv = ROPE_BASE ** (-jnp.linspace(0.0, 1.0, RET_QK_DIM // 2, dtype=F32))
    ret_ang = t.astype(F32)[:, None] * ret_inv[None, :]
    rc, rs = jnp.cos(ret_ang), jnp.sin(ret_ang)
    row = jnp.where(is_tok, r // GRID_W, 0).astype(F32)
    col = jnp.where(is_tok, r % GRID_W, 0).astype(F32)
    ax_half = ATT_HEAD_DIM // 2
    ax_inv = ROPE_BASE ** (-jnp.arange(ax_half // 2, dtype=F32) * 2.0 / ax_half)
    ra, ca = row[:, None] * ax_inv[None, :], col[:, None] * ax_inv[None, :]
    z = jnp.zeros_like(ra)
    return jnp.concatenate([
        rc, rc,
        -rs, rs,
        jnp.cos(ra), jnp.cos(ra), jnp.cos(ca), jnp.cos(ca),
        -jnp.sin(ra), z, -jnp.sin(ca), z,
        z, jnp.sin(ra), z, jnp.sin(ca),
    ], axis=1)


class _Rows:
    def __init__(self, tm, nt, s, counts, first_layer):
        self.tm, self.nt, self.s, self.counts, self.first_layer = tm, nt, s, counts, first_layer

    def specs(self):
        tm, nt = self.tm, self.nt
        if not self.first_layer:
            return [pl.BlockSpec((1, tm, D_MODEL), lambda i, j: (i, j, 0))]
        n0 = self.counts[0]

        def first(i, j):
            mine = i < n0
            return (jnp.where(mine, i, n0 - 1), jnp.where(mine, j, nt - 1), 0)

        def second(i, j):
            mine = i >= n0
            return (jnp.where(mine, i - n0, 0), jnp.where(mine, j, 0), 0)

        return [pl.BlockSpec((1, tm, D_MODEL), first), pl.BlockSpec((1, tm, D_MODEL), second),
                _resident((N_META, D_MODEL))]

    def n_refs(self):
        return 3 if self.first_layer else 1

    def load(self, refs):
        if not self.first_layer:
            return refs[0][0]
        xa_ref, xb_ref, meta_ref = refs
        i, j = pl.program_id(0), pl.program_id(1)
        x = jnp.where(i < self.counts[0], xa_ref[0], xb_ref[0])
        meta_at = self.s - (self.nt - 1) * self.tm
        assert meta_at + N_META == self.tm
        meta_tile = jnp.concatenate([jnp.zeros((meta_at, D_MODEL), F32), meta_ref[...]], axis=0)
        row = j * self.tm + lax.broadcasted_iota(jnp.int32, (self.tm, 1), 0)
        return jnp.where(row < self.s, x, meta_tile)


def _row_inputs(rows, h, xs, meta):
    return [xs[0], xs[1], meta] if rows.first_layer else [h]


def _in_proj_kernel(rows, *refs):
    row_refs, refs = refs[:rows.n_refs()], refs[rows.n_refs():]
    gain_ref, w_ref, tab_ref, qg_ref, kg_ref = refs[:5]
    outs = dict(zip(IN_NAMES, refs[5:]))
    offsets = {}
    lo = 0
    for name, width in zip(IN_NAMES, IN_SPLITS):
        offsets[name] = (lo, width)
        lo += width

    def tab(i):
        return tab_ref[:, i * LANES:(i + 1) * LANES]

    def ret_rope(x, scale):
        y = x * tab(TAB_RC) + pltpu.roll(x, RET_QK_DIM // 2, 1) * tab(TAB_RS)
        return y if scale is None else y * scale

    def axial(x, gain, scale):
        y = _rms_rows(x) * gain
        y = (y * tab(TAB_AC) + pltpu.roll(y, LANES - 32, 1) * tab(TAB_ASP)
             + pltpu.roll(y, 32, 1) * tab(TAB_ASM))
        return y if scale is None else y * scale

    x = rows.load(row_refs)
    xg = (x * gain_ref[...]).astype(BF16)
    inv_rms = lax.rsqrt(jnp.mean(x * x, axis=-1, keepdims=True) + NORM_EPS)

    def proj(name):
        lo, width = offsets[name]
        return jnp.dot(xg, w_ref[:, lo:lo + width], preferred_element_type=F32) * inv_rms

    def heads(name, fn):
        y = proj(name)
        for hd in range(y.shape[1] // LANES):
            sl = slice(hd * LANES, (hd + 1) * LANES)
            outs[name][0, :, sl] = fn(y[:, sl]).astype(outs[name].dtype)

    heads("aq", lambda y: axial(y, qg_ref[...], ATT_HEAD_DIM ** -0.5))
    heads("ak", lambda y: axial(y, kg_ref[...], None))
    heads("rq", lambda y: ret_rope(y, None))
    heads("rk", lambda y: ret_rope(y, RET_QK_DIM ** -0.5))
    for name in ("av", "rv", "rg", "gr", "ga"):
        outs[name][0] = proj(name).astype(outs[name].dtype)


def _in_proj(rows, h, xs, meta, gain, w_in, tabs, q_gain, k_gain, b):
    tm, nt = rows.tm, rows.nt

    def out_rows(w):
        return pl.BlockSpec((1, tm, w), lambda i, j: (i, j, 0))

    return pl.pallas_call(
        functools.partial(_in_proj_kernel, rows),
        grid=(b, nt),
        in_specs=rows.specs() + [_resident((1, D_MODEL)), _resident((D_MODEL, IN_WIDTH)),
                                 pl.BlockSpec((tm, TAB_W), lambda i, j: (j, 0)),
                                 _resident((1, LANES)), _resident((1, LANES))],
        out_specs=[out_rows(w) for w in IN_SPLITS],
        out_shape=[jax.ShapeDtypeStruct((b, nt * tm, w), BF16) for w in IN_SPLITS],
        compiler_params=_params(2),
        name="in_proj",
    )(*_row_inputs(rows, h, xs, meta), gain, w_in, tabs, q_gain, k_gain)


(COEF_DECAY, COEF_Q_FWD, COEF_Q_BWD, COEF_K_FWD, COEF_K_BWD,
 COEF_Q_FWD_META, COEF_Q_BWD_META, COEF_K_FWD_META, COEF_K_BWD_META) = range(9)
N_COEF = 9
FWD, BWD = 0, 1


def _retention_kernel(lg_ref, q_ref, k_ref, v_ref, g_ref, o_ref, coef_ref, state_ref, stack_ref):
    s_rows = q_ref.shape[1] - N_META
    nc = s_rows // BLOCK + 1
    row = lax.broadcasted_iota(jnp.int32, (BLOCK, BLOCK), 0).astype(F32)
    col = lax.broadcasted_iota(jnp.int32, (BLOCK, BLOCK), 1).astype(F32)
    rel = row - col
    meta_pos = jnp.minimum(row + PAD, BLOCK - 1.0)
    chunk_decay = []
    for hd in range(RET_HEADS):
        log_f, log_b = lg_ref[FWD, hd], lg_ref[BWD, hd]
        coef_ref[hd, COEF_DECAY] = jnp.where(rel >= 0, jnp.exp(log_f * jnp.maximum(rel, 0.0)),
                                             jnp.exp(log_b * jnp.maximum(-rel, 0.0)))
        for pos, shift in ((row, 0), (meta_pos, COEF_Q_FWD_META - COEF_Q_FWD)):
            coef_ref[hd, COEF_Q_FWD + shift] = jnp.exp(log_f * (pos + 1.0))
            coef_ref[hd, COEF_Q_BWD + shift] = jnp.exp(log_b * (BLOCK - pos))
            coef_ref[hd, COEF_K_FWD + shift] = jnp.exp(log_f * (BLOCK - 1.0 - pos))
            coef_ref[hd, COEF_K_BWD + shift] = jnp.exp(log_b * pos)
        ones_row = jnp.ones((1, RET_V_DIM), F32)
        chunk_decay.append((jnp.exp(log_f * BLOCK * ones_row), jnp.exp(log_b * BLOCK * ones_row)))
    state_ref[...] = jnp.zeros_like(state_ref)

    def coef_shift(c):
        return COEF_Q_FWD_META - COEF_Q_FWD if c == 0 else 0

    def load(ref, c, cols):
        if c == 0:
            meta = ref[0, s_rows:s_rows + N_META, cols]
            return jnp.concatenate([meta, jnp.zeros((PAD, meta.shape[1]), meta.dtype)], axis=0)
        return ref[0, (c - 1) * BLOCK:c * BLOCK, cols]

    def qk_cols(hd):
        return slice(hd * RET_QK_DIM, (hd + 1) * RET_QK_DIM)

    def v_cols(hd):
        return slice(hd * RET_V_DIM, (hd + 1) * RET_V_DIM)

    def stack_rows(c, hd, direction=None):
        base = (c * RET_HEADS + hd) * 2 * BLOCK
        if direction is None:
            return slice(base, base + 2 * BLOCK)
        return slice(base + direction * BLOCK, base + (direction + 1) * BLOCK)

    def scan_step(c, direction, update):
        shift = coef_shift(c)
        for hd in range(RET_HEADS):
            slot = direction * RET_HEADS + hd
            state = state_ref[slot]
            stack_ref[stack_rows(c, hd, direction), :] = state.astype(stack_ref.dtype)
            if update:
                k = (load(k_ref, c, qk_cols(hd)).astype(F32)
                     * coef_ref[hd, COEF_K_FWD + direction + shift])
                pushed = lax.dot_general(k.astype(BF16), load(v_ref, c, v_cols(hd)),
                                         (((0,), (0,)), ((), ())), preferred_element_type=F32)
                state_ref[slot] = state * chunk_decay[hd][direction] + pushed

    for t in range(nc):
        scan_step(t, FWD, update=t < nc - 1)
        scan_step(nc - 1 - t, BWD, update=t < nc - 1)

    for c in range(nc):
        shift = coef_shift(c)
        for hd in range(RET_HEADS):
            q, k, v = load(q_ref, c, qk_cols(hd)), load(k_ref, c, qk_cols(hd)), load(v_ref, c, v_cols(hd))
            s = lax.dot_general(q, k, (((1,), (1,)), ((), ())), preferred_element_type=F32)
            qf = q.astype(F32)
            lhs = jnp.concatenate([(s * coef_ref[hd, COEF_DECAY]).astype(BF16),
                                   (qf * coef_ref[hd, COEF_Q_FWD + shift]).astype(BF16),
                                   (qf * coef_ref[hd, COEF_Q_BWD + shift]).astype(BF16)], axis=1)
            rhs = jnp.concatenate([v, stack_ref[stack_rows(c, hd), :]], axis=0)
            o = _rms_rows(jnp.dot(lhs, rhs, preferred_element_type=F32))
            g = load(g_ref, c, v_cols(hd)).astype(F32)
            o = (o * (g * _sigmoid(g))).astype(o_ref.dtype)
            if c == 0:
                o_ref[0, s_rows:s_rows + N_META, v_cols(hd)] = o[:N_META]
            else:
                o_ref[0, (c - 1) * BLOCK:c * BLOCK, v_cols(hd)] = o


def _retention(log_gamma, rq, rk, rv, rg):
    b, lh, _ = rq.shape
    nc = (lh - N_META) // BLOCK + 1

    def seq(w):
        return pl.BlockSpec((1, lh, w), lambda i, lg: (i, 0, 0))

    return pl.pallas_call(
        _retention_kernel,
        grid_spec=pltpu.PrefetchScalarGridSpec(
            num_scalar_prefetch=1,
            grid=(b,),
            in_specs=[seq(RET_QK_W), seq(RET_QK_W), seq(RET_V_W), seq(RET_V_W)],
            out_specs=seq(RET_V_W),
            scratch_shapes=[pltpu.VMEM((RET_HEADS, N_COEF, BLOCK, BLOCK), F32),
                            pltpu.VMEM((2 * RET_HEADS, RET_QK_DIM, RET_V_DIM), F32),
                            pltpu.VMEM((nc * RET_HEADS * 2 * BLOCK, RET_V_DIM), BF16)],
        ),
        out_shape=jax.ShapeDtypeStruct((b, lh, RET_V_W), BF16),
        compiler_params=_params(1),
        name="retention",
    )(log_gamma, rq, rk, rv, rg)


def _attention_kernel(q_ref, k_ref, v_ref, o_ref):
    tq = q_ref.shape[1]
    lh = k_ref.shape[1]
    lp = lh + PAD
    sub = _row_tile(tq, ATT_SUB_ROWS)
    key = lax.broadcasted_iota(jnp.int32, (1, lp), 1)
    zeros = jnp.zeros((PAD, ATT_HEAD_DIM), k_ref.dtype)
    k = jnp.concatenate([k_ref[0], zeros], axis=0)
    v1 = jnp.concatenate([jnp.concatenate([v_ref[0], zeros], axis=0),
                          jnp.ones((lp, LANES), v_ref.dtype)], axis=1)
    for r0 in range(0, tq, sub):
        for g in range(ATT_GROUP):
            rows, cols = slice(r0, r0 + sub), slice(g * LANES, (g + 1) * LANES)
            s = lax.dot_general(q_ref[0, rows, cols], k, (((1,), (1,)), ((), ())),
                                preferred_element_type=F32)
            s = jnp.where(key < lh, s, MASK_VALUE)
            p = jnp.exp(s - jnp.max(s, axis=-1, keepdims=True))
            ov = jnp.dot(p.astype(BF16), v1, preferred_element_type=F32)
            o = ov[:, :ATT_HEAD_DIM] * (1.0 / ov[:, ATT_HEAD_DIM:])
            o_ref[0, rows, cols] = o.astype(o_ref.dtype)


def _attention(aq, ak, av):
    b, lh, _ = aq.shape
    tq = min(ATT_Q_ROWS, lh)
    return pl.pallas_call(
        _attention_kernel,
        grid=(b, ATT_KV_HEADS, pl.cdiv(lh, tq)),
        in_specs=[pl.BlockSpec((1, tq, ATT_GROUP_W), lambda i, j, t: (i, t, j)),
                  pl.BlockSpec((1, lh, ATT_HEAD_DIM), lambda i, j, t: (i, 0, j)),
                  pl.BlockSpec((1, lh, ATT_HEAD_DIM), lambda i, j, t: (i, 0, j))],
        out_specs=pl.BlockSpec((1, tq, ATT_GROUP_W), lambda i, j, t: (i, t, j)),
        out_shape=jax.ShapeDtypeStruct((b, lh, ATT_Q_W), BF16),
        compiler_params=_params(3),
        name="attention",
    )(aq, ak, av)


def _mix_out_kernel(rows, *refs):
    row_refs, refs = refs[:rows.n_refs()], refs[rows.n_refs():]
    ret_ref, att_ref, gr_ref, ga_ref, wr_ref, wa_ref, wo_ref, o_ref = refs
    r = jnp.dot(ret_ref[0], wr_ref[...], preferred_element_type=F32)
    a = jnp.dot(att_ref[0], wa_ref[...], preferred_element_type=F32)
    merged = _sigmoid(gr_ref[0].astype(F32)) * r + _sigmoid(ga_ref[0].astype(F32)) * a
    o_ref[0] = rows.load(row_refs) + jnp.dot(merged.astype(BF16), wo_ref[...],
                                             preferred_element_type=F32)


def _mix_out(rows, h, xs, meta, ret, att, gr, ga, w_ret_o, w_att_o, w_out, b):
    tm, nt = rows.tm, rows.nt
    tile = pl.BlockSpec((1, tm, D_MODEL), lambda i, j: (i, j, 0))
    w = _resident((D_MODEL, D_MODEL))
    return pl.pallas_call(
        functools.partial(_mix_out_kernel, rows),
        grid=(b, nt),
        in_specs=rows.specs() + [tile] * 4 + [w] * 3,
        out_specs=tile,
        out_shape=jax.ShapeDtypeStruct((b, nt * tm, D_MODEL), F32),
        input_output_aliases={} if rows.first_layer else {0: 0},
        compiler_params=_params(2),
        name="mix_out",
    )(*_row_inputs(rows, h, xs, meta), ret, att, gr, ga, w_ret_o, w_att_o, w_out)


def _ffn_kernel(final_counts, h_ref, gain_ref, wi_ref, wo_ref, *rest):
    x = h_ref[0]
    xg = (x * gain_ref[...]).astype(BF16)
    inv_rms = lax.rsqrt(jnp.mean(x * x, axis=-1, keepdims=True) + NORM_EPS)
    a = jnp.dot(xg, wi_ref[:, :D_FF], preferred_element_type=F32) * inv_rms
    u = jnp.dot(xg, wi_ref[:, D_FF:], preferred_element_type=F32) * inv_rms
    act = (a * _sigmoid(a) * u).astype(BF16)
    out = x + jnp.dot(act, wo_ref[...], preferred_element_type=F32)
    if final_counts is None:
        (o_ref,) = rest
        o_ref[0] = out
        return
    fgain_ref, ya_ref, yb_ref = rest
    y = _rms_rows(out) * fgain_ref[...]

    @pl.when(pl.program_id(0) < final_counts[0])
    def _():
        ya_ref[0] = y

    @pl.when(pl.program_id(0) >= final_counts[0])
    def _():
        yb_ref[0] = y


def _ffn(h, gain, w_ffn_in, w_ffn_out, tm, final=None):
    b, lh, _ = h.shape
    nt = lh // tm
    tile = pl.BlockSpec((1, tm, D_MODEL), lambda i, j: (i, j, 0))
    in_specs = [tile, _resident((1, D_MODEL)), _resident((D_MODEL, 2 * D_FF)),
                _resident((D_FF, D_MODEL))]
    if final is None:
        return pl.pallas_call(
            functools.partial(_ffn_kernel, None),
            grid=(b, nt),
            in_specs=in_specs,
            out_specs=tile,
            out_shape=jax.ShapeDtypeStruct(h.shape, F32),
            input_output_aliases={0: 0},
            compiler_params=_params(2),
            name="ffn",
        )(h, gain, w_ffn_in, w_ffn_out)
    fgain, counts, s = final
    n0 = counts[0]

    def first(i, j):
        mine = i < n0
        return (jnp.where(mine, i, n0 - 1), jnp.where(mine, j, nt - 1), 0)

    def second(i, j):
        mine = i >= n0
        return (jnp.where(mine, i - n0, 0), jnp.where(mine, j, 0), 0)

    return pl.pallas_call(
        functools.partial(_ffn_kernel, counts),
        grid=(b, nt),
        in_specs=in_specs + [_resident((1, D_MODEL))],
        out_specs=[pl.BlockSpec((1, tm, D_MODEL), first), pl.BlockSpec((1, tm, D_MODEL), second)],
        out_shape=[jax.ShapeDtypeStruct((c, s, D_MODEL), F32) for c in counts],
        compiler_params=_params(2),
        name="ffn_final",
    )(h, gain, w_ffn_in, w_ffn_out, fgain)


def _trunk(xs, meta_tokens, norm_mix, w_in, ret_decay, q_norm, k_norm, w_ret_o, w_att_o, w_out,
           norm_ffn, w_ffn_in, w_ffn_out, norm_final):
    assert len(xs) == 2
    s = xs[0].shape[1]
    assert all(x.shape[1:] == (s, D_MODEL) for x in xs) and s % GRID_W == 0 and s % BLOCK == 0
    lh = s + N_META
    counts = tuple(x.shape[0] for x in xs)
    b = sum(counts)
    tm = _row_tile(lh)
    nt = lh // tm
    assert s > (nt - 1) * tm, "the meta tokens must sit in the last row tile"
    meta = meta_tokens.astype(F32)
    tabs = _rotary_tables(s)
    gamma = 1.0 - jnp.exp2(-ret_decay.astype(F32))
    log_gamma = jnp.log(gamma)
    h = None
    for l in range(DEPTH):
        rows = _Rows(tm, nt, s, counts, first_layer=l == 0)
        rq, rk, rv, rg, aq, ak, av, gr, ga = _in_proj(
            rows, h, xs, meta, norm_mix[l][None], w_in[l].astype(BF16), tabs, q_norm[l][None],
            k_norm[l][None], b)
        ret = _retention(log_gamma[l], rq, rk, rv, rg)
        att = _attention(aq, ak, av)
        h = _mix_out(rows, h, xs, meta, ret, att, gr, ga, w_ret_o[l].astype(BF16),
                     w_att_o[l].astype(BF16), w_out[l].astype(BF16), b)
        final = (norm_final[None], counts, s) if l == DEPTH - 1 else None
        h = _ffn(h, norm_ffn[l][None], w_ffn_in[l].astype(BF16), w_ffn_out[l].astype(BF16), tm,
                 final)
    return h


def kernel(x_prompt, x_sample, meta_tokens, norm_mix, w_in, ret_decay, q_norm, k_norm, w_ret_o,
           w_att_o, w_out, norm_ffn, w_ffn_in, w_ffn_out, norm_final):
    y_prompt, y_sample = _trunk([x_prompt, x_sample], meta_tokens, norm_mix, w_in, ret_decay,
                                q_norm, k_norm, w_ret_o, w_att_o, w_out, norm_ffn, w_ffn_in,
                                w_ffn_out, norm_final)
    return (y_prompt, y_sample)
```

```python
import functools

import jax
import jax.numpy as jnp
from jax import lax
from jax.experimental import pallas as pl
from jax.experimental.pallas import tpu as pltpu

D_MODEL = 1024
DEPTH = 2
N_META = 16
GRID_W = 64
BLOCK = 128
PAD = BLOCK - N_META
RET_HEADS = 4
RET_QK_DIM = 128
RET_V_DIM = 256
ATT_HEADS = 8
ATT_KV_HEADS = 2
ATT_GROUP = ATT_HEADS // ATT_KV_HEADS
ATT_HEAD_DIM = 128
D_FF = ((8 * D_MODEL + 3 * 256 - 1) // (3 * 256)) * 256
ROPE_BASE = 10000.0
NORM_EPS = 1e-6
RET_QK_W = RET_HEADS * RET_QK_DIM
RET_V_W = RET_HEADS * RET_V_DIM
ATT_Q_W = ATT_HEADS * ATT_HEAD_DIM
ATT_KV_W = ATT_KV_HEADS * ATT_HEAD_DIM
ATT_GROUP_W = ATT_GROUP * ATT_HEAD_DIM
IN_NAMES = ("rq", "rk", "rv", "rg", "aq", "ak", "av", "gr", "ga")
IN_SPLITS = (RET_QK_W, RET_QK_W, RET_V_W, RET_V_W, ATT_Q_W, ATT_KV_W, ATT_KV_W, D_MODEL, D_MODEL)
IN_WIDTH = sum(IN_SPLITS)

LANES = 128
BF16_SUBLANES = 16
VMEM_BYTES_V7X = 64 * 1024 * 1024
VMEM_LIMIT = VMEM_BYTES_V7X * 7 // 8
ROW_TILE_TARGET = 688
ATT_Q_ROWS = 1088
ATT_SUB_ROWS = 272
MASK_VALUE = -1e30

F32 = jnp.float32
BF16 = jnp.bfloat16

TAB_RC, TAB_RS, TAB_AC, TAB_ASP, TAB_ASM = range(5)
TAB_W = 5 * LANES


def _row_tile(rows, target=None):
    target = ROW_TILE_TARGET if target is None else target
    best = None
    for d in range(BF16_SUBLANES, min(rows, target) + 1, BF16_SUBLANES):
        if rows % d == 0:
            best = d
    assert best is not None, rows
    return best


def _params(n_axes):
    return pltpu.CompilerParams(dimension_semantics=("arbitrary",) * n_axes,
                                vmem_limit_bytes=VMEM_LIMIT)


def _resident(shape):
    return pl.BlockSpec(shape, lambda *_: (0,) * len(shape), pipeline_mode=pl.Buffered(1))


def _rms_rows(x):
    return x * lax.rsqrt(jnp.mean(x * x, axis=-1, keepdims=True) + NORM_EPS)


def _sigmoid(x):
    return 1.0 / (1.0 + jnp.exp(-x))


def _rotary_tables(s):
    r = jnp.arange(s + N_META, dtype=jnp.int32)
    is_tok = r < s
    t = jnp.where(is_tok, r + N_META, r - s)
    ret_inv = ROPE_BASE ** (-jnp.linspace(0.0, 1.0, RET_QK_DIM // 2, dtype=F32))
    ret_ang = t.astype(F32)[:, None] * ret_inv[None, :]
    rc, rs = jnp.cos(ret_ang), jnp.sin(ret_ang)
    row = jnp.where(is_tok, r // GRID_W, 0).astype(F32)
    col = jnp.where(is_tok, r % GRID_W, 0).astype(F32)
    ax_half = ATT_HEAD_DIM // 2
    ax_inv = ROPE_BASE ** (-jnp.arange(ax_half // 2, dtype=F32) * 2.0 / ax_half)
    ra, ca = row[:, None] * ax_inv[None, :], col[:, None] * ax_inv[None, :]
    z = jnp.zeros_like(ra)
    return jnp.concatenate([
        rc, rc,
        -rs, rs,
        jnp.cos(ra), jnp.cos(ra), jnp.cos(ca), jnp.cos(ca),
        -jnp.sin(ra), z, -jnp.sin(ca), z,
        z, jnp.sin(ra), z, jnp.sin(ca),
    ], axis=1)


class _Rows:
    def __init__(self, tm, nt, s, counts, first_layer):
        self.tm, self.nt, self.s, self.counts, self.first_layer = tm, nt, s, counts, first_layer

    def specs(self):
        tm, nt = self.tm, self.nt
        if not self.first_layer:
            return [pl.BlockSpec((1, tm, D_MODEL), lambda i, j: (i, j, 0))]
        n0 = self.counts[0]

        def first(i, j):
            mine = i < n0
            return (jnp.where(mine, i, n0 - 1), jnp.where(mine, j, nt - 1), 0)

        def second(i, j):
            mine = i >= n0
            return (jnp.where(mine, i - n0, 0), jnp.where(mine, j, 0), 0)

        return [pl.BlockSpec((1, tm, D_MODEL), first), pl.BlockSpec((1, tm, D_MODEL), second),
                _resident((N_META, D_MODEL))]

    def n_refs(self):
        return 3 if self.first_layer else 1

    def load(self, refs):
        if not self.first_layer:
            return refs[0][0]
        xa_ref, xb_ref, meta_ref = refs
        i, j = pl.program_id(0), pl.program_id(1)
        x = jnp.where(i < self.counts[0], xa_ref[0], xb_ref[0])
        meta_at = self.s - (self.nt - 1) * self.tm
        assert meta_at + N_META == self.tm
        meta_tile = jnp.concatenate([jnp.zeros((meta_at, D_MODEL), F32), meta_ref[...]], axis=0)
        row = j * self.tm + lax.broadcasted_iota(jnp.int32, (self.tm, 1), 0)
        return jnp.where(row < self.s, x, meta_tile)


def _row_inputs(rows, h, xs, meta):
    return [xs[0], xs[1], meta] if rows.first_layer else [h]


def _in_proj_kernel(rows, *refs):
    row_refs, refs = refs[:rows.n_refs()], refs[rows.n_refs():]
    gain_ref, w_ref, tab_ref, qg_ref, kg_ref = refs[:5]
    outs = dict(zip(IN_NAMES, refs[5:]))
    offsets = {}
    lo = 0
    for name, width in zip(IN_NAMES, IN_SPLITS):
        offsets[name] = (lo, width)
        lo += width

    def tab(i):
        return tab_ref[:, i * LANES:(i + 1) * LANES]

    def ret_rope(x, scale):
        y = x * tab(TAB_RC) + pltpu.roll(x, RET_QK_DIM // 2, 1) * tab(TAB_RS)
        return y if scale is None else y * scale

    def axial(x, gain, scale):
        y = _rms_rows(x) * gain
        y = (y * tab(TAB_AC) + pltpu.roll(y, LANES - 32, 1) * tab(TAB_ASP)
             + pltpu.roll(y, 32, 1) * tab(TAB_ASM))
        return y if scale is None else y * scale

    x = rows.load(row_refs)
    xg = (x * gain_ref[...]).astype(BF16)
    inv_rms = lax.rsqrt(jnp.mean(x * x, axis=-1, keepdims=True) + NORM_EPS)

    def proj(name):
        lo, width = offsets[name]
        return jnp.dot(xg, w_ref[:, lo:lo + width], preferred_element_type=F32) * inv_rms

    def heads(name, fn):
        y = proj(name)
        for hd in range(y.shape[1] // LANES):
            sl = slice(hd * LANES, (hd + 1) * LANES)
            outs[name][0, :, sl] = fn(y[:, sl]).astype(outs[name].dtype)

    heads("aq", lambda y: axial(y, qg_ref[...], ATT_HEAD_DIM ** -0.5))
    heads("ak", lambda y: axial(y, kg_ref[...], None))
    heads("rq", lambda y: ret_rope(y, None))
    heads("rk", lambda y: ret_rope(y, RET_QK_DIM ** -0.5))
    for name in ("av", "rv", "rg", "gr", "ga"):
        outs[name][0] = proj(name).astype(outs[name].dtype)


def _in_proj(rows, h, xs, meta, gain, w_in, tabs, q_gain, k_gain, b):
    tm, nt = rows.tm, rows.nt

    def out_rows(w):
        return pl.BlockSpec((1, tm, w), lambda i, j: (i, j, 0))

    return pl.pallas_call(
        functools.partial(_in_proj_kernel, rows),
        grid=(b, nt),
        in_specs=rows.specs() + [_resident((1, D_MODEL)), _resident((D_MODEL, IN_WIDTH)),
                                 pl.BlockSpec((tm, TAB_W), lambda i, j: (j, 0)),
                                 _resident((1, LANES)), _resident((1, LANES))],
        out_specs=[out_rows(w) for w in IN_SPLITS],
        out_shape=[jax.ShapeDtypeStruct((b, nt * tm, w), BF16) for w in IN_SPLITS],
        compiler_params=_params(2),
        name="in_proj",
    )(*_row_inputs(rows, h, xs, meta), gain, w_in, tabs, q_gain, k_gain)


(COEF_DECAY, COEF_Q_FWD, COEF_Q_BWD, COEF_K_FWD, COEF_K_BWD,
 COEF_Q_FWD_META, COEF_Q_BWD_META, COEF_K_FWD_META, COEF_K_BWD_META) = range(9)
N_COEF = 9
FWD, BWD = 0, 1


def _retention_kernel(lg_ref, q_ref, k_ref, v_ref, g_ref, o_ref, coef_ref, state_ref, stack_ref):
    s_rows = q_ref.shape[1] - N_META
    nc = s_rows // BLOCK + 1
    row = lax.broadcasted_iota(jnp.int32, (BLOCK, BLOCK), 0).astype(F32)
    col = lax.broadcasted_iota(jnp.int32, (BLOCK, BLOCK), 1).astype(F32)
    rel = row - col
    meta_pos = jnp.minimum(row + PAD, BLOCK - 1.0)
    chunk_decay = []
    for hd in range(RET_HEADS):
        log_f, log_b = lg_ref[FWD, hd], lg_ref[BWD, hd]
        coef_ref[hd, COEF_DECAY] = jnp.where(rel >= 0, jnp.exp(log_f * jnp.maximum(rel, 0.0)),
                                             jnp.exp(log_b * jnp.maximum(-rel, 0.0)))
        for pos, shift in ((row, 0), (meta_pos, COEF_Q_FWD_META - COEF_Q_FWD)):
            coef_ref[hd, COEF_Q_FWD + shift] = jnp.exp(log_f * (pos + 1.0))
            coef_ref[hd, COEF_Q_BWD + shift] = jnp.exp(log_b * (BLOCK - pos))
            coef_ref[hd, COEF_K_FWD + shift] = jnp.exp(log_f * (BLOCK - 1.0 - pos))
            coef_ref[hd, COEF_K_BWD + shift] = jnp.exp(log_b * pos)
        ones_row = jnp.ones((1, RET_V_DIM), F32)
        chunk_decay.append((jnp.exp(log_f * BLOCK * ones_row), jnp.exp(log_b * BLOCK * ones_row)))
    state_ref[...] = jnp.zeros_like(state_ref)

    def coef_shift(c):
        return COEF_Q_FWD_META - COEF_Q_FWD if c == 0 else 0

    def load(ref, c, cols):
        if c == 0:
            meta = ref[0, s_rows:s_rows + N_META, cols]
            return jnp.concatenate([meta, jnp.zeros((PAD, meta.shape[1]), meta.dtype)], axis=0)
        return ref[0, (c - 1) * BLOCK:c * BLOCK, cols]

    def qk_cols(hd):
        return slice(hd * RET_QK_DIM, (hd + 1) * RET_QK_DIM)

    def v_cols(hd):
        return slice(hd * RET_V_DIM, (hd + 1) * RET_V_DIM)

    def stack_rows(c, hd, direction=None):
        base = (c * RET_HEADS + hd) * 2 * BLOCK
        if direction is None:
            return slice(base, base + 2 * BLOCK)
        return slice(base + direction * BLOCK, base + (direction + 1) * BLOCK)

    def scan_step(c, direction, update):
        shift = coef_shift(c)
        for hd in range(RET_HEADS):
            slot = direction * RET_HEADS + hd
            state = state_ref[slot]
            stack_ref[stack_rows(c, hd, direction), :] = state.astype(stack_ref.dtype)
            if update:
                k = (load(k_ref, c, qk_cols(hd)).astype(F32)
                     * coef_ref[hd, COEF_K_FWD + direction + shift])
                pushed = lax.dot_general(k.astype(BF16), load(v_ref, c, v_cols(hd)),
                                         (((0,), (0,)), ((), ())), preferred_element_type=F32)
                state_ref[slot] = state * chunk_decay[hd][direction] + pushed

    for t in range(nc):
        scan_step(t, FWD, update=t < nc - 1)
        scan_step(nc - 1 - t, BWD, update=t < nc - 1)

    for c in range(nc):
        shift = coef_shift(c)
        for hd in range(RET_HEADS):
            q, k, v = load(q_ref, c, qk_cols(hd)), load(k_ref, c, qk_cols(hd)), load(v_ref, c, v_cols(hd))
            s = lax.dot_general(q, k, (((1,), (1,)), ((), ())), preferred_element_type=F32)
            qf = q.astype(F32)
            lhs = jnp.concatenate([(s * coef_ref[hd, COEF_DECAY]).astype(BF16),
                                   (qf * coef_ref[hd, COEF_Q_FWD + shift]).astype(BF16),
                                   (qf * coef_ref[hd, COEF_Q_BWD + shift]).astype(BF16)], axis=1)
            rhs = jnp.concatenate([v, stack_ref[stack_rows(c, hd), :]], axis=0)
            o = _rms_rows(jnp.dot(lhs, rhs, preferred_element_type=F32))
            g = load(g_ref, c, v_cols(hd)).astype(F32)
            o = (o * (g * _sigmoid(g))).astype(o_ref.dtype)
            if c == 0:
                o_ref[0, s_rows:s_rows + N_META, v_cols(hd)] = o[:N_META]
            else:
                o_ref[0, (c - 1) * BLOCK:c * BLOCK, v_cols(hd)] = o


def _retention(log_gamma, rq, rk, rv, rg):
    b, lh, _ = rq.shape
    nc = (lh - N_META) // BLOCK + 1

    def seq(w):
        return pl.BlockSpec((1, lh, w), lambda i, lg: (i, 0, 0))

    return pl.pallas_call(
        _retention_kernel,
        grid_spec=pltpu.PrefetchScalarGridSpec(
            num_scalar_prefetch=1,
            grid=(b,),
            in_specs=[seq(RET_QK_W), seq(RET_QK_W), seq(RET_V_W), seq(RET_V_W)],
            out_specs=seq(RET_V_W),
            scratch_shapes=[pltpu.VMEM((RET_HEADS, N_COEF, BLOCK, BLOCK), F32),
                            pltpu.VMEM((2 * RET_HEADS, RET_QK_DIM, RET_V_DIM), F32),
                            pltpu.VMEM((nc * RET_HEADS * 2 * BLOCK, RET_V_DIM), BF16)],
        ),
        out_shape=jax.ShapeDtypeStruct((b, lh, RET_V_W), BF16),
        compiler_params=_params(1),
        name="retention",
    )(log_gamma, rq, rk, rv, rg)


def _attention_kernel(q_ref, k_ref, v_ref, o_ref):
    tq = q_ref.shape[1]
    lh = k_ref.shape[1]
    lp = lh + PAD
    sub = _row_tile(tq, ATT_SUB_ROWS)
    key = lax.broadcasted_iota(jnp.int32, (1, lp), 1)
    zeros = jnp.zeros((PAD, ATT_HEAD_DIM), k_ref.dtype)
    k = jnp.concatenate([k_ref[0], zeros], axis=0)
    v1 = jnp.concatenate([jnp.concatenate([v_ref[0], zeros], axis=0),
                          jnp.ones((lp, LANES), v_ref.dtype)], axis=1)
    chains = [(slice(r0, r0 + sub), slice(g * LANES, (g + 1) * LANES))
              for r0 in range(0, tq, sub) for g in range(ATT_GROUP)]

    def scores(c):
        rows, cols = chains[c]
        s = lax.dot_general(q_ref[0, rows, cols], k, (((1,), (1,)), ((), ())),
                            preferred_element_type=F32)
        return jnp.where(key < lh, s, MASK_VALUE)

    s_next = scores(0)
    for c, (rows, cols) in enumerate(chains):
        s = s_next
        if c + 1 < len(chains):
            s_next = scores(c + 1)
        p = jnp.exp(s - jnp.max(s, axis=-1, keepdims=True))
        ov = jnp.dot(p.astype(BF16), v1, preferred_element_type=F32)
        o = ov[:, :ATT_HEAD_DIM] * (1.0 / ov[:, ATT_HEAD_DIM:])
        o_ref[0, rows, cols] = o.astype(o_ref.dtype)


def _attention(aq, ak, av):
    b, lh, _ = aq.shape
    tq = min(ATT_Q_ROWS, lh)
    return pl.pallas_call(
        _attention_kernel,
        grid=(b, ATT_KV_HEADS, pl.cdiv(lh, tq)),
        in_specs=[pl.BlockSpec((1, tq, ATT_GROUP_W), lambda i, j, t: (i, t, j)),
                  pl.BlockSpec((1, lh, ATT_HEAD_DIM), lambda i, j, t: (i, 0, j)),
                  pl.BlockSpec((1, lh, ATT_HEAD_DIM), lambda i, j, t: (i, 0, j))],
        out_specs=pl.BlockSpec((1, tq, ATT_GROUP_W), lambda i, j, t: (i, t, j)),
        out_shape=jax.ShapeDtypeStruct((b, lh, ATT_Q_W), BF16),
        compiler_params=_params(3),
        name="attention",
    )(aq, ak, av)


def _mix_out_kernel(rows, *refs):
    row_refs, refs = refs[:rows.n_refs()], refs[rows.n_refs():]
    ret_ref, att_ref, gr_ref, ga_ref, wr_ref, wa_ref, wo_ref, o_ref = refs
    r = jnp.dot(ret_ref[0], wr_ref[...], preferred_element_type=F32)
    a = jnp.dot(att_ref[0], wa_ref[...], preferred_element_type=F32)
    merged = _sigmoid(gr_ref[0].astype(F32)) * r + _sigmoid(ga_ref[0].astype(F32)) * a
    o_ref[0] = rows.load(row_refs) + jnp.dot(merged.astype(BF16), wo_ref[...],
                                             preferred_element_type=F32)


def _mix_out(rows, h, xs, meta, ret, att, gr, ga, w_ret_o, w_att_o, w_out, b):
    tm, nt = rows.tm, rows.nt
    tile = pl.BlockSpec((1, tm, D_MODEL), lambda i, j: (i, j, 0))
    w = _resident((D_MODEL, D_MODEL))
    return pl.pallas_call(
        functools.partial(_mix_out_kernel, rows),
        grid=(b, nt),
        in_specs=rows.specs() + [tile] * 4 + [w] * 3,
        out_specs=tile,
        out_shape=jax.ShapeDtypeStruct((b, nt * tm, D_MODEL), F32),
        input_output_aliases={} if rows.first_layer else {0: 0},
        compiler_params=_params(2),
        name="mix_out",
    )(*_row_inputs(rows, h, xs, meta), ret, att, gr, ga, w_ret_o, w_att_o, w_out)


def _ffn_kernel(final_counts, h_ref, gain_ref, wi_ref, wo_ref, *rest):
    x = h_ref[0]
    xg = (x * gain_ref[...]).astype(BF16)
    inv_rms = lax.rsqrt(jnp.mean(x * x, axis=-1, keepdims=True) + NORM_EPS)
    a = jnp.dot(xg, wi_ref[:, :D_FF], preferred_element_type=F32) * inv_rms
    u = jnp.dot(xg, wi_ref[:, D_FF:], preferred_element_type=F32) * inv_rms
    act = (a * _sigmoid(a) * u).astype(BF16)
    out = x + jnp.dot(act, wo_ref[...], preferred_element_type=F32)
    if final_counts is None:
        (o_ref,) = rest
        o_ref[0] = out
        return
    fgain_ref, ya_ref, yb_ref = rest
    y = _rms_rows(out) * fgain_ref[...]

    @pl.when(pl.program_id(0) < final_counts[0])
    def _():
        ya_ref[0] = y

    @pl.when(pl.program_id(0) >= final_counts[0])
    def _():
        yb_ref[0] = y


def _ffn(h, gain, w_ffn_in, w_ffn_out, tm, final=None):
    b, lh, _ = h.shape
    nt = lh // tm
    tile = pl.BlockSpec((1, tm, D_MODEL), lambda i, j: (i, j, 0))
    in_specs = [tile, _resident((1, D_MODEL)), _resident((D_MODEL, 2 * D_FF)),
                _resident((D_FF, D_MODEL))]
    if final is None:
        return pl.pallas_call(
            functools.partial(_ffn_kernel, None),
            grid=(b, nt),
            in_specs=in_specs,
            out_specs=tile,
            out_shape=jax.ShapeDtypeStruct(h.shape, F32),
            input_output_aliases={0: 0},
            compiler_params=_params(2),
            name="ffn",
        )(h, gain, w_ffn_in, w_ffn_out)
    fgain, counts, s = final
    n0 = counts[0]

    def first(i, j):
        mine = i < n0
        return (jnp.where(mine, i, n0 - 1), jnp.where(mine, j, nt - 1), 0)

    def second(i, j):
        mine = i >= n0
        return (jnp.where(mine, i - n0, 0), jnp.where(mine, j, 0), 0)

    return pl.pallas_call(
        functools.partial(_ffn_kernel, counts),
        grid=(b, nt),
        in_specs=in_specs + [_resident((1, D_MODEL))],
        out_specs=[pl.BlockSpec((1, tm, D_MODEL), first), pl.BlockSpec((1, tm, D_MODEL), second)],
        out_shape=[jax.ShapeDtypeStruct((c, s, D_MODEL), F32) for c in counts],
        compiler_params=_params(2),
        name="ffn_final",
    )(h, gain, w_ffn_in, w_ffn_out, fgain)


def _trunk(xs, meta_tokens, norm_mix, w_in, ret_decay, q_norm, k_norm, w_ret_o, w_att_o, w_out,
           norm_ffn, w_ffn_in, w_ffn_out, norm_final):
    assert len(xs) == 2
    s = xs[0].shape[1]
    assert all(x.shape[1:] == (s, D_MODEL) for x in xs) and s % GRID_W == 0 and s % BLOCK == 0
    lh = s + N_META
    counts = tuple(x.shape[0] for x in xs)
    b = sum(counts)
    tm = _row_tile(lh)
    nt = lh // tm
    assert s > (nt - 1) * tm, "the meta tokens must sit in the last row tile"
    meta = meta_tokens.astype(F32)
    tabs = _rotary_tables(s)
    gamma = 1.0 - jnp.exp2(-ret_decay.astype(F32))
    log_gamma = jnp.log(gamma)
    h = None
    for l in range(DEPTH):
        rows = _Rows(tm, nt, s, counts, first_layer=l == 0)
        rq, rk, rv, rg, aq, ak, av, gr, ga = _in_proj(
            rows, h, xs, meta, norm_mix[l][None], w_in[l].astype(BF16), tabs, q_norm[l][None],
            k_norm[l][None], b)
        ret = _retention(log_gamma[l], rq, rk, rv, rg)
        att = _attention(aq, ak, av)
        h = _mix_out(rows, h, xs, meta, ret, att, gr, ga, w_ret_o[l].astype(BF16),
                     w_att_o[l].astype(BF16), w_out[l].astype(BF16), b)
        final = (norm_final[None], counts, s) if l == DEPTH - 1 else None
        h = _ffn(h, norm_ffn[l][None], w_ffn_in[l].astype(BF16), w_ffn_out[l].astype(BF16), tm,
                 final)
    return h


def kernel(x_prompt, x_sample, meta_tokens, norm_mix, w_in, ret_decay, q_norm, k_norm, w_ret_o,
           w_att_o, w_out, norm_ffn, w_ffn_in, w_ffn_out, norm_final):
    y_prompt, y_sample = _trunk([x_prompt, x_sample], meta_tokens, norm_mix, w_in, ret_decay,
                                q_norm, k_norm, w_ret_o, w_att_o, w_out, norm_ffn, w_ffn_in,
                                w_ffn_out, norm_final)
    return (y_prompt, y_sample)
```

```python
import functools

import jax
import jax.numpy as jnp
from jax import lax
from jax.experimental import pallas as pl
from jax.experimental.pallas import tpu as pltpu

D_MODEL = 1024
DEPTH = 2
N_META = 16
GRID_W = 64
BLOCK = 128
PAD = BLOCK - N_META
RET_HEADS = 4
RET_QK_DIM = 128
RET_V_DIM = 256
ATT_HEADS = 8
ATT_KV_HEADS = 2
ATT_GROUP = ATT_HEADS // ATT_KV_HEADS
ATT_HEAD_DIM = 128
D_FF = ((8 * D_MODEL + 3 * 256 - 1) // (3 * 256)) * 256
ROPE_BASE = 10000.0
NORM_EPS = 1e-6
RET_QK_W = RET_HEADS * RET_QK_DIM
RET_V_W = RET_HEADS * RET_V_DIM
ATT_Q_W = ATT_HEADS * ATT_HEAD_DIM
ATT_KV_W = ATT_KV_HEADS * ATT_HEAD_DIM
ATT_GROUP_W = ATT_GROUP * ATT_HEAD_DIM
IN_NAMES = ("rq", "rk", "rv", "rg", "aq", "ak", "av", "gr", "ga")
IN_SPLITS = (RET_QK_W, RET_QK_W, RET_V_W, RET_V_W, ATT_Q_W, ATT_KV_W, ATT_KV_W, D_MODEL, D_MODEL)
IN_WIDTH = sum(IN_SPLITS)

LANES = 128
BF16_SUBLANES = 16
VMEM_BYTES_V7X = 64 * 1024 * 1024
VMEM_LIMIT = VMEM_BYTES_V7X * 7 // 8
ROW_TILE_TARGET = 688
ATT_Q_ROWS = 1056
ATT_SUB_ROWS = 352
MASK_VALUE = -1e30

F32 = jnp.float32
BF16 = jnp.bfloat16

TAB_RC, TAB_RS, TAB_AC, TAB_ASP, TAB_ASM = range(5)
TAB_W = 5 * LANES


def _row_tile(rows, target=None):
    target = ROW_TILE_TARGET if target is None else target
    best = None
    for d in range(BF16_SUBLANES, min(rows, target) + 1, BF16_SUBLANES):
        if rows % d == 0:
            best = d
    assert best is not None, rows
    return best


def _params(n_axes):
    return pltpu.CompilerParams(dimension_semantics=("arbitrary",) * n_axes,
                                vmem_limit_bytes=VMEM_LIMIT)


def _resident(shape):
    return pl.BlockSpec(shape, lambda *_: (0,) * len(shape), pipeline_mode=pl.Buffered(1))


def _rms_rows(x):
    return x * lax.rsqrt(jnp.mean(x * x, axis=-1, keepdims=True) + NORM_EPS)


def _sigmoid(x):
    return 1.0 / (1.0 + jnp.exp(-x))


def _rotary_tables(s):
    r = jnp.arange(s + N_META, dtype=jnp.int32)
    is_tok = r < s
    t = jnp.where(is_tok, r + N_META, r - s)
    ret_inv = ROPE_BASE ** (-jnp.linspace(0.0, 1.0, RET_QK_DIM // 2, dtype=F32))
    ret_ang = t.astype(F32)[:, None] * ret_inv[None, :]
    rc, rs = jnp.cos(ret_ang), jnp.sin(ret_ang)
    row = jnp.where(is_tok, r // GRID_W, 0).astype(F32)
    col = jnp.where(is_tok, r % GRID_W, 0).astype(F32)
    ax_half = ATT_HEAD_DIM // 2
    ax_inv = ROPE_BASE ** (-jnp.arange(ax_half // 2, dtype=F32) * 2.0 / ax_half)
    ra, ca = row[:, None] * ax_inv[None, :], col[:, None] * ax_inv[None, :]
    z = jnp.zeros_like(ra)
    return jnp.concatenate([
        rc, rc,
        -rs, rs,
        jnp.cos(ra), jnp.cos(ra), jnp.cos(ca), jnp.cos(ca),
        -jnp.sin(ra), z, -jnp.sin(ca), z,
        z, jnp.sin(ra), z, jnp.sin(ca),
    ], axis=1)


class _Rows:
    def __init__(self, tm, nt, s, counts, first_layer):
        self.tm, self.nt, self.s, self.counts, self.first_layer = tm, nt, s, counts, first_layer

    def specs(self):
        tm, nt = self.tm, self.nt
        if not self.first_layer:
            return [pl.BlockSpec((1, tm, D_MODEL), lambda i, j: (i, j, 0))]
        n0 = self.counts[0]

        def first(i, j):
            mine = i < n0
            return (jnp.where(mine, i, n0 - 1), jnp.where(mine, j, nt - 1), 0)

        def second(i, j):
            mine = i >= n0
            return (jnp.where(mine, i - n0, 0), jnp.where(mine, j, 0), 0)

        return [pl.BlockSpec((1, tm, D_MODEL), first), pl.BlockSpec((1, tm, D_MODEL), second),
                _resident((N_META, D_MODEL))]

    def n_refs(self):
        return 3 if self.first_layer else 1

    def load(self, refs):
        if not self.first_layer:
            return refs[0][0]
        xa_ref, xb_ref, meta_ref = refs
        i, j = pl.program_id(0), pl.program_id(1)
        x = jnp.where(i < self.counts[0], xa_ref[0], xb_ref[0])
        meta_at = self.s - (self.nt - 1) * self.tm
        assert meta_at + N_META == self.tm
        meta_tile = jnp.concatenate([jnp.zeros((meta_at, D_MODEL), F32), meta_ref[...]], axis=0)
        row = j * self.tm + lax.broadcasted_iota(jnp.int32, (self.tm, 1), 0)
        return jnp.where(row < self.s, x, meta_tile)


def _row_inputs(rows, h, xs, meta):
    return [xs[0], xs[1], meta] if rows.first_layer else [h]


def _in_proj_kernel(rows, *refs):
    row_refs, refs = refs[:rows.n_refs()], refs[rows.n_refs():]
    gain_ref, w_ref, tab_ref, qg_ref, kg_ref = refs[:5]
    outs = dict(zip(IN_NAMES, refs[5:]))
    offsets = {}
    lo = 0
    for name, width in zip(IN_NAMES, IN_SPLITS):
        offsets[name] = (lo, width)
        lo += width

    def tab(i):
        return tab_ref[:, i * LANES:(i + 1) * LANES]

    def ret_rope(x, scale):
        y = x * tab(TAB_RC) + pltpu.roll(x, RET_QK_DIM // 2, 1) * tab(TAB_RS)
        return y if scale is None else y * scale

    def axial(x, gain, scale):
        y = _rms_rows(x) * gain
        y = (y * tab(TAB_AC) + pltpu.roll(y, LANES - 32, 1) * tab(TAB_ASP)
             + pltpu.roll(y, 32, 1) * tab(TAB_ASM))
        return y if scale is None else y * scale

    x = rows.load(row_refs)
    xg = (x * gain_ref[...]).astype(BF16)
    inv_rms = lax.rsqrt(jnp.mean(x * x, axis=-1, keepdims=True) + NORM_EPS)

    def proj(name):
        lo, width = offsets[name]
        return jnp.dot(xg, w_ref[:, lo:lo + width], preferred_element_type=F32) * inv_rms

    def heads(name, fn):
        y = proj(name)
        for hd in range(y.shape[1] // LANES):
            sl = slice(hd * LANES, (hd + 1) * LANES)
            outs[name][0, :, sl] = fn(y[:, sl]).astype(outs[name].dtype)

    heads("aq", lambda y: axial(y, qg_ref[...], ATT_HEAD_DIM ** -0.5))
    heads("ak", lambda y: axial(y, kg_ref[...], None))
    heads("rq", lambda y: ret_rope(y, None))
    heads("rk", lambda y: ret_rope(y, RET_QK_DIM ** -0.5))
    for name in ("av", "rv", "rg", "gr", "ga"):
        outs[name][0] = proj(name).astype(outs[name].dtype)


def _in_proj(rows, h, xs, meta, gain, w_in, tabs, q_gain, k_gain, b):
    tm, nt = rows.tm, rows.nt

    def out_rows(w):
        return pl.BlockSpec((1, tm, w), lambda i, j: (i, j, 0))

    return pl.pallas_call(
        functools.partial(_in_proj_kernel, rows),
        grid=(b, nt),
        in_specs=rows.specs() + [_resident((1, D_MODEL)), _resident((D_MODEL, IN_WIDTH)),
                                 pl.BlockSpec((tm, TAB_W), lambda i, j: (j, 0)),
                                 _resident((1, LANES)), _resident((1, LANES))],
        out_specs=[out_rows(w) for w in IN_SPLITS],
        out_shape=[jax.ShapeDtypeStruct((b, nt * tm, w), BF16) for w in IN_SPLITS],
        compiler_params=_params(2),
        name="in_proj",
    )(*_row_inputs(rows, h, xs, meta), gain, w_in, tabs, q_gain, k_gain)


(COEF_DECAY, COEF_Q_FWD, COEF_Q_BWD, COEF_K_FWD, COEF_K_BWD,
 COEF_Q_FWD_META, COEF_Q_BWD_META, COEF_K_FWD_META, COEF_K_BWD_META) = range(9)
N_COEF = 9
FWD, BWD = 0, 1


def _retention_kernel(lg_ref, q_ref, k_ref, v_ref, g_ref, o_ref, coef_ref, state_ref, stack_ref):
    s_rows = q_ref.shape[1] - N_META
    nc = s_rows // BLOCK + 1
    row = lax.broadcasted_iota(jnp.int32, (BLOCK, BLOCK), 0).astype(F32)
    col = lax.broadcasted_iota(jnp.int32, (BLOCK, BLOCK), 1).astype(F32)
    rel = row - col
    meta_pos = jnp.minimum(row + PAD, BLOCK - 1.0)
    chunk_decay = []
    for hd in range(RET_HEADS):
        log_f, log_b = lg_ref[FWD, hd], lg_ref[BWD, hd]
        coef_ref[hd, COEF_DECAY] = jnp.where(rel >= 0, jnp.exp(log_f * jnp.maximum(rel, 0.0)),
                                             jnp.exp(log_b * jnp.maximum(-rel, 0.0)))
        for pos, shift in ((row, 0), (meta_pos, COEF_Q_FWD_META - COEF_Q_FWD)):
            coef_ref[hd, COEF_Q_FWD + shift] = jnp.exp(log_f * (pos + 1.0))
            coef_ref[hd, COEF_Q_BWD + shift] = jnp.exp(log_b * (BLOCK - pos))
            coef_ref[hd, COEF_K_FWD + shift] = jnp.exp(log_f * (BLOCK - 1.0 - pos))
            coef_ref[hd, COEF_K_BWD + shift] = jnp.exp(log_b * pos)
        ones_row = jnp.ones((1, RET_V_DIM), F32)
        chunk_decay.append((jnp.exp(log_f * BLOCK * ones_row), jnp.exp(log_b * BLOCK * ones_row)))
    state_ref[...] = jnp.zeros_like(state_ref)

    def coef_shift(c):
        return COEF_Q_FWD_META - COEF_Q_FWD if c == 0 else 0

    def load(ref, c, cols):
        if c == 0:
            meta = ref[0, s_rows:s_rows + N_META, cols]
            return jnp.concatenate([meta, jnp.zeros((PAD, meta.shape[1]), meta.dtype)], axis=0)
        return ref[0, (c - 1) * BLOCK:c * BLOCK, cols]

    def qk_cols(hd):
        return slice(hd * RET_QK_DIM, (hd + 1) * RET_QK_DIM)

    def v_cols(hd):
        return slice(hd * RET_V_DIM, (hd + 1) * RET_V_DIM)

    def stack_rows(c, hd, direction=None):
        base = (c * RET_HEADS + hd) * 2 * BLOCK
        if direction is None:
            return slice(base, base + 2 * BLOCK)
        return slice(base + direction * BLOCK, base + (direction + 1) * BLOCK)

    def scan_step(c, direction, update):
        shift = coef_shift(c)
        for hd in range(RET_HEADS):
            slot = direction * RET_HEADS + hd
            state = state_ref[slot]
            stack_ref[stack_rows(c, hd, direction), :] = state.astype(stack_ref.dtype)
            if update:
                k = (load(k_ref, c, qk_cols(hd)).astype(F32)
                     * coef_ref[hd, COEF_K_FWD + direction + shift])
                pushed = lax.dot_general(k.astype(BF16), load(v_ref, c, v_cols(hd)),
                                         (((0,), (0,)), ((), ())), preferred_element_type=F32)
                state_ref[slot] = state * chunk_decay[hd][direction] + pushed

    for t in range(nc):
        scan_step(t, FWD, update=t < nc - 1)
        scan_step(nc - 1 - t, BWD, update=t < nc - 1)

    for c in range(nc):
        shift = coef_shift(c)
        for hd in range(RET_HEADS):
            q, k, v = load(q_ref, c, qk_cols(hd)), load(k_ref, c, qk_cols(hd)), load(v_ref, c, v_cols(hd))
            s = lax.dot_general(q, k, (((1,), (1,)), ((), ())), preferred_element_type=F32)
            qf = q.astype(F32)
            lhs = jnp.concatenate([(s * coef_ref[hd, COEF_DECAY]).astype(BF16),
                                   (qf * coef_ref[hd, COEF_Q_FWD + shift]).astype(BF16),
                                   (qf * coef_ref[hd, COEF_Q_BWD + shift]).astype(BF16)], axis=1)
            rhs = jnp.concatenate([v, stack_ref[stack_rows(c, hd), :]], axis=0)
            o = _rms_rows(jnp.dot(lhs, rhs, preferred_element_type=F32))
            g = load(g_ref, c, v_cols(hd)).astype(F32)
            o = (o * (g * _sigmoid(g))).astype(o_ref.dtype)
            if c == 0:
                o_ref[0, s_rows:s_rows + N_META, v_cols(hd)] = o[:N_META]
            else:
                o_ref[0, (c - 1) * BLOCK:c * BLOCK, v_cols(hd)] = o


def _retention(log_gamma, rq, rk, rv, rg):
    b, lh, _ = rq.shape
    nc = (lh - N_META) // BLOCK + 1

    def seq(w):
        return pl.BlockSpec((1, lh, w), lambda i, lg: (i, 0, 0))

    return pl.pallas_call(
        _retention_kernel,
        grid_spec=pltpu.PrefetchScalarGridSpec(
            num_scalar_prefetch=1,
            grid=(b,),
            in_specs=[seq(RET_QK_W), seq(RET_QK_W), seq(RET_V_W), seq(RET_V_W)],
            out_specs=seq(RET_V_W),
            scratch_shapes=[pltpu.VMEM((RET_HEADS, N_COEF, BLOCK, BLOCK), F32),
                            pltpu.VMEM((2 * RET_HEADS, RET_QK_DIM, RET_V_DIM), F32),
                            pltpu.VMEM((nc * RET_HEADS * 2 * BLOCK, RET_V_DIM), BF16)],
        ),
        out_shape=jax.ShapeDtypeStruct((b, lh, RET_V_W), BF16),
        compiler_params=_params(1),
        name="retention",
    )(log_gamma, rq, rk, rv, rg)


def _attention_kernel(q_ref, k_ref, v_ref, o_ref):
    tq = q_ref.shape[1]
    lh = k_ref.shape[1]
    lp = lh + PAD
    sub = _row_tile(tq, ATT_SUB_ROWS)
    key = lax.broadcasted_iota(jnp.int32, (1, lp), 1)
    zeros = jnp.zeros((PAD, ATT_HEAD_DIM), k_ref.dtype)
    k = jnp.concatenate([k_ref[0], zeros], axis=0)
    v1 = jnp.concatenate([jnp.concatenate([v_ref[0], zeros], axis=0),
                          jnp.ones((lp, LANES), v_ref.dtype)], axis=1)
    chains = [(slice(r0, r0 + sub), slice(g * LANES, (g + 1) * LANES))
              for r0 in range(0, tq, sub) for g in range(ATT_GROUP)]

    def scores(c):
        rows, cols = chains[c]
        s = lax.dot_general(q_ref[0, rows, cols], k, (((1,), (1,)), ((), ())),
                            preferred_element_type=F32)
        return jnp.where(key < lh, s, MASK_VALUE)

    s_next = scores(0)
    for c, (rows, cols) in enumerate(chains):
        s = s_next
        if c + 1 < len(chains):
            s_next = scores(c + 1)
        p = jnp.exp(s - jnp.max(s, axis=-1, keepdims=True))
        ov = jnp.dot(p.astype(BF16), v1, preferred_element_type=F32)
        o = ov[:, :ATT_HEAD_DIM] * (1.0 / ov[:, ATT_HEAD_DIM:])
        o_ref[0, rows, cols] = o.astype(o_ref.dtype)


def _attention(aq, ak, av):
    b, lh, _ = aq.shape
    tq = min(ATT_Q_ROWS, lh)
    return pl.pallas_call(
        _attention_kernel,
        grid=(b, ATT_KV_HEADS, pl.cdiv(lh, tq)),
        in_specs=[pl.BlockSpec((1, tq, ATT_GROUP_W), lambda i, j, t: (i, t, j)),
                  pl.BlockSpec((1, lh, ATT_HEAD_DIM), lambda i, j, t: (i, 0, j)),
                  pl.BlockSpec((1, lh, ATT_HEAD_DIM), lambda i, j, t: (i, 0, j))],
        out_specs=pl.BlockSpec((1, tq, ATT_GROUP_W), lambda i, j, t: (i, t, j)),
        out_shape=jax.ShapeDtypeStruct((b, lh, ATT_Q_W), BF16),
        compiler_params=_params(3),
        name="attention",
    )(aq, ak, av)


def _mix_out_kernel(rows, *refs):
    row_refs, refs = refs[:rows.n_refs()], refs[rows.n_refs():]
    ret_ref, att_ref, gr_ref, ga_ref, wr_ref, wa_ref, wo_ref, o_ref = refs
    r = jnp.dot(ret_ref[0], wr_ref[...], preferred_element_type=F32)
    a = jnp.dot(att_ref[0], wa_ref[...], preferred_element_type=F32)
    merged = _sigmoid(gr_ref[0].astype(F32)) * r + _sigmoid(ga_ref[0].astype(F32)) * a
    o_ref[0] = rows.load(row_refs) + jnp.dot(merged.astype(BF16), wo_ref[...],
                                             preferred_element_type=F32)


def _mix_out(rows, h, xs, meta, ret, att, gr, ga, w_ret_o, w_att_o, w_out, b):
    tm, nt = rows.tm, rows.nt
    tile = pl.BlockSpec((1, tm, D_MODEL), lambda i, j: (i, j, 0))
    w = _resident((D_MODEL, D_MODEL))
    return pl.pallas_call(
        functools.partial(_mix_out_kernel, rows),
        grid=(b, nt),
        in_specs=rows.specs() + [tile] * 4 + [w] * 3,
        out_specs=tile,
        out_shape=jax.ShapeDtypeStruct((b, nt * tm, D_MODEL), F32),
        input_output_aliases={} if rows.first_layer else {0: 0},
        compiler_params=_params(2),
        name="mix_out",
    )(*_row_inputs(rows, h, xs, meta), ret, att, gr, ga, w_ret_o, w_att_o, w_out)


def _ffn_kernel(final_counts, h_ref, gain_ref, wi_ref, wo_ref, *rest):
    x = h_ref[0]
    xg = (x * gain_ref[...]).astype(BF16)
    inv_rms = lax.rsqrt(jnp.mean(x * x, axis=-1, keepdims=True) + NORM_EPS)
    a = jnp.dot(xg, wi_ref[:, :D_FF], preferred_element_type=F32) * inv_rms
    u = jnp.dot(xg, wi_ref[:, D_FF:], preferred_element_type=F32) * inv_rms
    act = (a * _sigmoid(a) * u).astype(BF16)
    out = x + jnp.dot(act, wo_ref[...], preferred_element_type=F32)
    if final_counts is None:
        (o_ref,) = rest
        o_ref[0] = out
        return
    fgain_ref, ya_ref, yb_ref = rest
    y = _rms_rows(out) * fgain_ref[...]

    @pl.when(pl.program_id(0) < final_counts[0])
    def _():
        ya_ref[0] = y

    @pl.when(pl.program_id(0) >= final_counts[0])
    def _():
        yb_ref[0] = y


def _ffn(h, gain, w_ffn_in, w_ffn_out, tm, final=None):
    b, lh, _ = h.shape
    nt = lh // tm
    tile = pl.BlockSpec((1, tm, D_MODEL), lambda i, j: (i, j, 0))
    in_specs = [tile, _resident((1, D_MODEL)), _resident((D_MODEL, 2 * D_FF)),
                _resident((D_FF, D_MODEL))]
    if final is None:
        return pl.pallas_call(
            functools.partial(_ffn_kernel, None),
            grid=(b, nt),
            in_specs=in_specs,
            out_specs=tile,
            out_shape=jax.ShapeDtypeStruct(h.shape, F32),
            input_output_aliases={0: 0},
            compiler_params=_params(2),
            name="ffn",
        )(h, gain, w_ffn_in, w_ffn_out)
    fgain, counts, s = final
    n0 = counts[0]

    def first(i, j):
        mine = i < n0
        return (jnp.where(mine, i, n0 - 1), jnp.where(mine, j, nt - 1), 0)

    def second(i, j):
        mine = i >= n0
        return (jnp.where(mine, i - n0, 0), jnp.where(mine, j, 0), 0)

    return pl.pallas_call(
        functools.partial(_ffn_kernel, counts),
        grid=(b, nt),
        in_specs=in_specs + [_resident((1, D_MODEL))],
        out_specs=[pl.BlockSpec((1, tm, D_MODEL), first), pl.BlockSpec((1, tm, D_MODEL), second)],
        out_shape=[jax.ShapeDtypeStruct((c, s, D_MODEL), F32) for c in counts],
        compiler_params=_params(2),
        name="ffn_final",
    )(h, gain, w_ffn_in, w_ffn_out, fgain)


def _trunk(xs, meta_tokens, norm_mix, w_in, ret_decay, q_norm, k_norm, w_ret_o, w_att_o, w_out,
           norm_ffn, w_ffn_in, w_ffn_out, norm_final):
    assert len(xs) == 2
    s = xs[0].shape[1]
    assert all(x.shape[1:] == (s, D_MODEL) for x in xs) and s % GRID_W == 0 and s % BLOCK == 0
    lh = s + N_META
    counts = tuple(x.shape[0] for x in xs)
    b = sum(counts)
    tm = _row_tile(lh)
    nt = lh // tm
    assert s > (nt - 1) * tm, "the meta tokens must sit in the last row tile"
    meta = meta_tokens.astype(F32)
    tabs = _rotary_tables(s)
    gamma = 1.0 - jnp.exp2(-ret_decay.astype(F32))
    log_gamma = jnp.log(gamma)
    h = None
    for l in range(DEPTH):
        rows = _Rows(tm, nt, s, counts, first_layer=l == 0)
        rq, rk, rv, rg, aq, ak, av, gr, ga = _in_proj(
            rows, h, xs, meta, norm_mix[l][None], w_in[l].astype(BF16), tabs, q_norm[l][None],
            k_norm[l][None], b)
        ret = _retention(log_gamma[l], rq, rk, rv, rg)
        att = _attention(aq, ak, av)
        h = _mix_out(rows, h, xs, meta, ret, att, gr, ga, w_ret_o[l].astype(BF16),
                     w_att_o[l].astype(BF16), w_out[l].astype(BF16), b)
        final = (norm_final[None], counts, s) if l == DEPTH - 1 else None
        h = _ffn(h, norm_ffn[l][None], w_ffn_in[l].astype(BF16), w_ffn_out[l].astype(BF16), tm,
                 final)
    return h


def kernel(x_prompt, x_sample, meta_tokens, norm_mix, w_in, ret_decay, q_norm, k_norm, w_ret_o,
           w_att_o, w_out, norm_ffn, w_ffn_in, w_ffn_out, norm_final):
    y_prompt, y_sample = _trunk([x_prompt, x_sample], meta_tokens, norm_mix, w_in, ret_decay,
                                q_norm, k_norm, w_ret_o, w_att_o, w_out, norm_ffn, w_ffn_in,
                                w_ffn_out, norm_final)
    return (y_prompt, y_sample)
```

```python
import functools

import jax
import jax.numpy as jnp
from jax import lax
from jax.experimental import pallas as pl
from jax.experimental.pallas import tpu as pltpu

D_MODEL = 1024
DEPTH = 2
N_META = 16
GRID_W = 64
BLOCK = 128
PAD = BLOCK - N_META
RET_HEADS = 4
RET_QK_DIM = 128
RET_V_DIM = 256
ATT_HEADS = 8
ATT_KV_HEADS = 2
ATT_GROUP = ATT_HEADS // ATT_KV_HEADS
ATT_HEAD_DIM = 128
D_FF = ((8 * D_MODEL + 3 * 256 - 1) // (3 * 256)) * 256
ROPE_BASE = 10000.0
NORM_EPS = 1e-6
RET_QK_W = RET_HEADS * RET_QK_DIM
RET_V_W = RET_HEADS * RET_V_DIM
ATT_Q_W = ATT_HEADS * ATT_HEAD_DIM
ATT_KV_W = ATT_KV_HEADS * ATT_HEAD_DIM
ATT_GROUP_W = ATT_GROUP * ATT_HEAD_DIM
IN_NAMES = ("rq", "rk", "rv", "rg", "aq", "ak", "av", "gr", "ga")
IN_SPLITS = (RET_QK_W, RET_QK_W, RET_V_W, RET_V_W, ATT_Q_W, ATT_KV_W, ATT_KV_W, D_MODEL, D_MODEL)
IN_WIDTH = sum(IN_SPLITS)

LANES = 128
BF16_SUBLANES = 16
VMEM_BYTES_V7X = 64 * 1024 * 1024
VMEM_LIMIT = VMEM_BYTES_V7X * 7 // 8
ROW_TILE_TARGET = 688
ATT_Q_ROWS = 1056
ATT_SUB_ROWS = 352
FFN_CHUNK = 256
MASK_VALUE = -1e30

F32 = jnp.float32
BF16 = jnp.bfloat16

TAB_RC, TAB_RS, TAB_AC, TAB_ASP, TAB_ASM = range(5)
TAB_W = 5 * LANES


def _row_tile(rows, target=None):
    target = ROW_TILE_TARGET if target is None else target
    best = None
    for d in range(BF16_SUBLANES, min(rows, target) + 1, BF16_SUBLANES):
        if rows % d == 0:
            best = d
    assert best is not None, rows
    return best


def _params(n_axes):
    return pltpu.CompilerParams(dimension_semantics=("arbitrary",) * n_axes,
                                vmem_limit_bytes=VMEM_LIMIT)


def _resident(shape):
    return pl.BlockSpec(shape, lambda *_: (0,) * len(shape), pipeline_mode=pl.Buffered(1))


def _rms_rows(x):
    return x * lax.rsqrt(jnp.mean(x * x, axis=-1, keepdims=True) + NORM_EPS)


def _sigmoid(x):
    return 1.0 / (1.0 + jnp.exp(-x))


def _rotary_tables(s):
    r = jnp.arange(s + N_META, dtype=jnp.int32)
    is_tok = r < s
    t = jnp.where(is_tok, r + N_META, r - s)
    ret_inv = ROPE_BASE ** (-jnp.linspace(0.0, 1.0, RET_QK_DIM // 2, dtype=F32))
    ret_ang = t.astype(F32)[:, None] * ret_inv[None, :]
    rc, rs = jnp.cos(ret_ang), jnp.sin(ret_ang)
    row = jnp.where(is_tok, r // GRID_W, 0).astype(F32)
    col = jnp.where(is_tok, r % GRID_W, 0).astype(F32)
    ax_half = ATT_HEAD_DIM // 2
    ax_inv = ROPE_BASE ** (-jnp.arange(ax_half // 2, dtype=F32) * 2.0 / ax_half)
    ra, ca = row[:, None] * ax_inv[None, :], col[:, None] * ax_inv[None, :]
    z = jnp.zeros_like(ra)
    return jnp.concatenate([
        rc, rc,
        -rs, rs,
        jnp.cos(ra), jnp.cos(ra), jnp.cos(ca), jnp.cos(ca),
        -jnp.sin(ra), z, -jnp.sin(ca), z,
        z, jnp.sin(ra), z, jnp.sin(ca),
    ], axis=1)


class _Rows:
    def __init__(self, tm, nt, s, counts, first_layer):
        self.tm, self.nt, self.s, self.counts, self.first_layer = tm, nt, s, counts, first_layer

    def specs(self):
        tm, nt = self.tm, self.nt
        if not self.first_layer:
            return [pl.BlockSpec((1, tm, D_MODEL), lambda i, j: (i, j, 0))]
        n0 = self.counts[0]

        def first(i, j):
            mine = i < n0
            return (jnp.where(mine, i, n0 - 1), jnp.where(mine, j, nt - 1), 0)

        def second(i, j):
            mine = i >= n0
            return (jnp.where(mine, i - n0, 0), jnp.where(mine, j, 0), 0)

        return [pl.BlockSpec((1, tm, D_MODEL), first), pl.BlockSpec((1, tm, D_MODEL), second),
                _resident((N_META, D_MODEL))]

    def n_refs(self):
        return 3 if self.first_layer else 1

    def load(self, refs):
        if not self.first_layer:
            return refs[0][0]
        xa_ref, xb_ref, meta_ref = refs
        i, j = pl.program_id(0), pl.program_id(1)
        x = jnp.where(i < self.counts[0], xa_ref[0], xb_ref[0])
        meta_at = self.s - (self.nt - 1) * self.tm
        assert meta_at + N_META == self.tm
        meta_tile = jnp.concatenate([jnp.zeros((meta_at, D_MODEL), F32), meta_ref[...]], axis=0)
        row = j * self.tm + lax.broadcasted_iota(jnp.int32, (self.tm, 1), 0)
        return jnp.where(row < self.s, x, meta_tile)


def _row_inputs(rows, h, xs, meta):
    return [xs[0], xs[1], meta] if rows.first_layer else [h]


def _in_proj_kernel(rows, *refs):
    row_refs, refs = refs[:rows.n_refs()], refs[rows.n_refs():]
    gain_ref, w_ref, tab_ref, qg_ref, kg_ref = refs[:5]
    outs = dict(zip(IN_NAMES, refs[5:]))
    offsets = {}
    lo = 0
    for name, width in zip(IN_NAMES, IN_SPLITS):
        offsets[name] = (lo, width)
        lo += width

    def tab(i):
        return tab_ref[:, i * LANES:(i + 1) * LANES]

    def ret_rope(x, scale):
        y = x * tab(TAB_RC) + pltpu.roll(x, RET_QK_DIM // 2, 1) * tab(TAB_RS)
        return y if scale is None else y * scale

    def axial(x, gain, scale):
        y = _rms_rows(x) * gain
        y = (y * tab(TAB_AC) + pltpu.roll(y, LANES - 32, 1) * tab(TAB_ASP)
             + pltpu.roll(y, 32, 1) * tab(TAB_ASM))
        return y if scale is None else y * scale

    x = rows.load(row_refs)
    xg = (x * gain_ref[...]).astype(BF16)
    inv_rms = lax.rsqrt(jnp.mean(x * x, axis=-1, keepdims=True) + NORM_EPS)

    def proj(name):
        lo, width = offsets[name]
        return jnp.dot(xg, w_ref[:, lo:lo + width], preferred_element_type=F32) * inv_rms

    def heads(name, fn):
        y = proj(name)
        for hd in range(y.shape[1] // LANES):
            sl = slice(hd * LANES, (hd + 1) * LANES)
            outs[name][0, :, sl] = fn(y[:, sl]).astype(outs[name].dtype)

    heads("aq", lambda y: axial(y, qg_ref[...], ATT_HEAD_DIM ** -0.5))
    heads("ak", lambda y: axial(y, kg_ref[...], None))
    heads("rq", lambda y: ret_rope(y, None))
    heads("rk", lambda y: ret_rope(y, RET_QK_DIM ** -0.5))
    for name in ("av", "rv", "rg", "gr", "ga"):
        outs[name][0] = proj(name).astype(outs[name].dtype)


def _in_proj(rows, h, xs, meta, gain, w_in, tabs, q_gain, k_gain, b):
    tm, nt = rows.tm, rows.nt

    def out_rows(w):
        return pl.BlockSpec((1, tm, w), lambda i, j: (i, j, 0))

    return pl.pallas_call(
        functools.partial(_in_proj_kernel, rows),
        grid=(b, nt),
        in_specs=rows.specs() + [_resident((1, D_MODEL)), _resident((D_MODEL, IN_WIDTH)),
                                 pl.BlockSpec((tm, TAB_W), lambda i, j: (j, 0)),
                                 _resident((1, LANES)), _resident((1, LANES))],
        out_specs=[out_rows(w) for w in IN_SPLITS],
        out_shape=[jax.ShapeDtypeStruct((b, nt * tm, w), BF16) for w in IN_SPLITS],
        compiler_params=_params(2),
        name="in_proj",
    )(*_row_inputs(rows, h, xs, meta), gain, w_in, tabs, q_gain, k_gain)


(COEF_DECAY, COEF_Q_FWD, COEF_Q_BWD, COEF_K_FWD, COEF_K_BWD,
 COEF_Q_FWD_META, COEF_Q_BWD_META, COEF_K_FWD_META, COEF_K_BWD_META) = range(9)
N_COEF = 9
FWD, BWD = 0, 1


def _retention_kernel(lg_ref, q_ref, k_ref, v_ref, g_ref, o_ref, coef_ref, state_ref, stack_ref):
    s_rows = q_ref.shape[1] - N_META
    nc = s_rows // BLOCK + 1
    row = lax.broadcasted_iota(jnp.int32, (BLOCK, BLOCK), 0).astype(F32)
    col = lax.broadcasted_iota(jnp.int32, (BLOCK, BLOCK), 1).astype(F32)
    rel = row - col
    meta_pos = jnp.minimum(row + PAD, BLOCK - 1.0)
    chunk_decay = []
    for hd in range(RET_HEADS):
        log_f, log_b = lg_ref[FWD, hd], lg_ref[BWD, hd]
        coef_ref[hd, COEF_DECAY] = jnp.where(rel >= 0, jnp.exp(log_f * jnp.maximum(rel, 0.0)),
                                             jnp.exp(log_b * jnp.maximum(-rel, 0.0)))
        for pos, shift in ((row, 0), (meta_pos, COEF_Q_FWD_META - COEF_Q_FWD)):
            coef_ref[hd, COEF_Q_FWD + shift] = jnp.exp(log_f * (pos + 1.0))
            coef_ref[hd, COEF_Q_BWD + shift] = jnp.exp(log_b * (BLOCK - pos))
            coef_ref[hd, COEF_K_FWD + shift] = jnp.exp(log_f * (BLOCK - 1.0 - pos))
            coef_ref[hd, COEF_K_BWD + shift] = jnp.exp(log_b * pos)
        ones_row = jnp.ones((1, RET_V_DIM), F32)
        chunk_decay.append((jnp.exp(log_f * BLOCK * ones_row), jnp.exp(log_b * BLOCK * ones_row)))
    state_ref[...] = jnp.zeros_like(state_ref)

    def coef_shift(c):
        return COEF_Q_FWD_META - COEF_Q_FWD if c == 0 else 0

    def load(ref, c, cols):
        if c == 0:
            meta = ref[0, s_rows:s_rows + N_META, cols]
            return jnp.concatenate([meta, jnp.zeros((PAD, meta.shape[1]), meta.dtype)], axis=0)
        return ref[0, (c - 1) * BLOCK:c * BLOCK, cols]

    def qk_cols(hd):
        return slice(hd * RET_QK_DIM, (hd + 1) * RET_QK_DIM)

    def v_cols(hd):
        return slice(hd * RET_V_DIM, (hd + 1) * RET_V_DIM)

    def stack_rows(c, hd, direction=None):
        base = (c * RET_HEADS + hd) * 2 * BLOCK
        if direction is None:
            return slice(base, base + 2 * BLOCK)
        return slice(base + direction * BLOCK, base + (direction + 1) * BLOCK)

    def scan_step(c, direction, update):
        shift = coef_shift(c)
        for hd in range(RET_HEADS):
            slot = direction * RET_HEADS + hd
            state = state_ref[slot]
            stack_ref[stack_rows(c, hd, direction), :] = state.astype(stack_ref.dtype)
            if update:
                k = (load(k_ref, c, qk_cols(hd)).astype(F32)
                     * coef_ref[hd, COEF_K_FWD + direction + shift])
                pushed = lax.dot_general(k.astype(BF16), load(v_ref, c, v_cols(hd)),
                                         (((0,), (0,)), ((), ())), preferred_element_type=F32)
                state_ref[slot] = state * chunk_decay[hd][direction] + pushed

    for t in range(nc):
        scan_step(t, FWD, update=t < nc - 1)
        scan_step(nc - 1 - t, BWD, update=t < nc - 1)

    for c in range(nc):
        shift = coef_shift(c)
        for hd in range(RET_HEADS):
            q, k, v = load(q_ref, c, qk_cols(hd)), load(k_ref, c, qk_cols(hd)), load(v_ref, c, v_cols(hd))
            s = lax.dot_general(q, k, (((1,), (1,)), ((), ())), preferred_element_type=F32)
            qf = q.astype(F32)
            lhs = jnp.concatenate([(s * coef_ref[hd, COEF_DECAY]).astype(BF16),
                                   (qf * coef_ref[hd, COEF_Q_FWD + shift]).astype(BF16),
                                   (qf * coef_ref[hd, COEF_Q_BWD + shift]).astype(BF16)], axis=1)
            rhs = jnp.concatenate([v, stack_ref[stack_rows(c, hd), :]], axis=0)
            o = _rms_rows(jnp.dot(lhs, rhs, preferred_element_type=F32))
            g = load(g_ref, c, v_cols(hd)).astype(F32)
            o = (o * (g * _sigmoid(g))).astype(o_ref.dtype)
            if c == 0:
                o_ref[0, s_rows:s_rows + N_META, v_cols(hd)] = o[:N_META]
            else:
                o_ref[0, (c - 1) * BLOCK:c * BLOCK, v_cols(hd)] = o


def _retention(log_gamma, rq, rk, rv, rg):
    b, lh, _ = rq.shape
    nc = (lh - N_META) // BLOCK + 1

    def seq(w):
        return pl.BlockSpec((1, lh, w), lambda i, lg: (i, 0, 0))

    return pl.pallas_call(
        _retention_kernel,
        grid_spec=pltpu.PrefetchScalarGridSpec(
            num_scalar_prefetch=1,
            grid=(b,),
            in_specs=[seq(RET_QK_W), seq(RET_QK_W), seq(RET_V_W), seq(RET_V_W)],
            out_specs=seq(RET_V_W),
            scratch_shapes=[pltpu.VMEM((RET_HEADS, N_COEF, BLOCK, BLOCK), F32),
                            pltpu.VMEM((2 * RET_HEADS, RET_QK_DIM, RET_V_DIM), F32),
                            pltpu.VMEM((nc * RET_HEADS * 2 * BLOCK, RET_V_DIM), BF16)],
        ),
        out_shape=jax.ShapeDtypeStruct((b, lh, RET_V_W), BF16),
        compiler_params=_params(1),
        name="retention",
    )(log_gamma, rq, rk, rv, rg)


def _attention_kernel(q_ref, k_ref, v_ref, o_ref):
    tq = q_ref.shape[1]
    lh = k_ref.shape[1]
    lp = lh + PAD
    sub = _row_tile(tq, ATT_SUB_ROWS)
    key = lax.broadcasted_iota(jnp.int32, (1, lp), 1)
    zeros = jnp.zeros((PAD, ATT_HEAD_DIM), k_ref.dtype)
    k = jnp.concatenate([k_ref[0], zeros], axis=0)
    v1 = jnp.concatenate([jnp.concatenate([v_ref[0], zeros], axis=0),
                          jnp.ones((lp, LANES), v_ref.dtype)], axis=1)
    chains = [(slice(r0, r0 + sub), slice(g * LANES, (g + 1) * LANES))
              for r0 in range(0, tq, sub) for g in range(ATT_GROUP)]

    def scores(c):
        rows, cols = chains[c]
        s = lax.dot_general(q_ref[0, rows, cols], k, (((1,), (1,)), ((), ())),
                            preferred_element_type=F32)
        return jnp.where(key < lh, s, MASK_VALUE)

    s_next = scores(0)
    for c, (rows, cols) in enumerate(chains):
        s = s_next
        if c + 1 < len(chains):
            s_next = scores(c + 1)
        p = jnp.exp(s - jnp.max(s, axis=-1, keepdims=True))
        ov = jnp.dot(p.astype(BF16), v1, preferred_element_type=F32)
        o = ov[:, :ATT_HEAD_DIM] * (1.0 / ov[:, ATT_HEAD_DIM:])
        o_ref[0, rows, cols] = o.astype(o_ref.dtype)


def _attention(aq, ak, av):
    b, lh, _ = aq.shape
    tq = min(ATT_Q_ROWS, lh)
    return pl.pallas_call(
        _attention_kernel,
        grid=(b, ATT_KV_HEADS, pl.cdiv(lh, tq)),
        in_specs=[pl.BlockSpec((1, tq, ATT_GROUP_W), lambda i, j, t: (i, t, j)),
                  pl.BlockSpec((1, lh, ATT_HEAD_DIM), lambda i, j, t: (i, 0, j)),
                  pl.BlockSpec((1, lh, ATT_HEAD_DIM), lambda i, j, t: (i, 0, j))],
        out_specs=pl.BlockSpec((1, tq, ATT_GROUP_W), lambda i, j, t: (i, t, j)),
        out_shape=jax.ShapeDtypeStruct((b, lh, ATT_Q_W), BF16),
        compiler_params=_params(3),
        name="attention",
    )(aq, ak, av)


def _mix_out_kernel(rows, *refs):
    row_refs, refs = refs[:rows.n_refs()], refs[rows.n_refs():]
    ret_ref, att_ref, gr_ref, ga_ref, wr_ref, wa_ref, wo_ref, o_ref = refs
    r = jnp.dot(ret_ref[0], wr_ref[...], preferred_element_type=F32)
    a = jnp.dot(att_ref[0], wa_ref[...], preferred_element_type=F32)
    merged = _sigmoid(gr_ref[0].astype(F32)) * r + _sigmoid(ga_ref[0].astype(F32)) * a
    o_ref[0] = rows.load(row_refs) + jnp.dot(merged.astype(BF16), wo_ref[...],
                                             preferred_element_type=F32)


def _mix_out(rows, h, xs, meta, ret, att, gr, ga, w_ret_o, w_att_o, w_out, b):
    tm, nt = rows.tm, rows.nt
    tile = pl.BlockSpec((1, tm, D_MODEL), lambda i, j: (i, j, 0))
    w = _resident((D_MODEL, D_MODEL))
    return pl.pallas_call(
        functools.partial(_mix_out_kernel, rows),
        grid=(b, nt),
        in_specs=rows.specs() + [tile] * 4 + [w] * 3,
        out_specs=tile,
        out_shape=jax.ShapeDtypeStruct((b, nt * tm, D_MODEL), F32),
        input_output_aliases={} if rows.first_layer else {0: 0},
        compiler_params=_params(2),
        name="mix_out",
    )(*_row_inputs(rows, h, xs, meta), ret, att, gr, ga, w_ret_o, w_att_o, w_out)


def _ffn_kernel(final_counts, h_ref, gain_ref, wi_ref, wo_ref, *rest):
    x = h_ref[0]
    xg = (x * gain_ref[...]).astype(BF16)
    inv_rms = lax.rsqrt(jnp.mean(x * x, axis=-1, keepdims=True) + NORM_EPS)
    def pair(j):
        lo = j * FFN_CHUNK
        return (jnp.dot(xg, wi_ref[:, lo:lo + FFN_CHUNK], preferred_element_type=F32),
                jnp.dot(xg, wi_ref[:, D_FF + lo:D_FF + lo + FFN_CHUNK],
                        preferred_element_type=F32))

    acts = []
    nxt = pair(0)
    for j in range(D_FF // FFN_CHUNK):
        a, u = nxt
        if j + 1 < D_FF // FFN_CHUNK:
            nxt = pair(j + 1)
        a = a * inv_rms
        acts.append((a * _sigmoid(a) * (u * inv_rms)).astype(BF16))
    act = jnp.concatenate(acts, axis=1)
    out = x + jnp.dot(act, wo_ref[...], preferred_element_type=F32)
    if final_counts is None:
        (o_ref,) = rest
        o_ref[0] = out
        return
    fgain_ref, ya_ref, yb_ref = rest
    y = _rms_rows(out) * fgain_ref[...]

    @pl.when(pl.program_id(0) < final_counts[0])
    def _():
        ya_ref[0] = y

    @pl.when(pl.program_id(0) >= final_counts[0])
    def _():
        yb_ref[0] = y


def _ffn(h, gain, w_ffn_in, w_ffn_out, tm, final=None):
    b, lh, _ = h.shape
    nt = lh // tm
    tile = pl.BlockSpec((1, tm, D_MODEL), lambda i, j: (i, j, 0))
    in_specs = [tile, _resident((1, D_MODEL)), _resident((D_MODEL, 2 * D_FF)),
                _resident((D_FF, D_MODEL))]
    if final is None:
        return pl.pallas_call(
            functools.partial(_ffn_kernel, None),
            grid=(b, nt),
            in_specs=in_specs,
            out_specs=tile,
            out_shape=jax.ShapeDtypeStruct(h.shape, F32),
            input_output_aliases={0: 0},
            compiler_params=_params(2),
            name="ffn",
        )(h, gain, w_ffn_in, w_ffn_out)
    fgain, counts, s = final
    n0 = counts[0]

    def first(i, j):
        mine = i < n0
        return (jnp.where(mine, i, n0 - 1), jnp.where(mine, j, nt - 1), 0)

    def second(i, j):
        mine = i >= n0
        return (jnp.where(mine, i - n0, 0), jnp.where(mine, j, 0), 0)

    return pl.pallas_call(
        functools.partial(_ffn_kernel, counts),
        grid=(b, nt),
        in_specs=in_specs + [_resident((1, D_MODEL))],
        out_specs=[pl.BlockSpec((1, tm, D_MODEL), first), pl.BlockSpec((1, tm, D_MODEL), second)],
        out_shape=[jax.ShapeDtypeStruct((c, s, D_MODEL), F32) for c in counts],
        compiler_params=_params(2),
        name="ffn_final",
    )(h, gain, w_ffn_in, w_ffn_out, fgain)


def _trunk(xs, meta_tokens, norm_mix, w_in, ret_decay, q_norm, k_norm, w_ret_o, w_att_o, w_out,
           norm_ffn, w_ffn_in, w_ffn_out, norm_final):
    assert len(xs) == 2
    s = xs[0].shape[1]
    assert all(x.shape[1:] == (s, D_MODEL) for x in xs) and s % GRID_W == 0 and s % BLOCK == 0
    lh = s + N_META
    counts = tuple(x.shape[0] for x in xs)
    b = sum(counts)
    tm = _row_tile(lh)
    nt = lh // tm
    assert s > (nt - 1) * tm, "the meta tokens must sit in the last row tile"
    meta = meta_tokens.astype(F32)
    tabs = _rotary_tables(s)
    gamma = 1.0 - jnp.exp2(-ret_decay.astype(F32))
    log_gamma = jnp.log(gamma)
    h = None
    for l in range(DEPTH):
        rows = _Rows(tm, nt, s, counts, first_layer=l == 0)
        rq, rk, rv, rg, aq, ak, av, gr, ga = _in_proj(
            rows, h, xs, meta, norm_mix[l][None], w_in[l].astype(BF16), tabs, q_norm[l][None],
            k_norm[l][None], b)
        ret = _retention(log_gamma[l], rq, rk, rv, rg)
        att = _attention(aq, ak, av)
        h = _mix_out(rows, h, xs, meta, ret, att, gr, ga, w_ret_o[l].astype(BF16),
                     w_att_o[l].astype(BF16), w_out[l].astype(BF16), b)
        final = (norm_final[None], counts, s) if l == DEPTH - 1 else None
        h = _ffn(h, norm_ffn[l][None], w_ffn_in[l].astype(BF16), w_ffn_out[l].astype(BF16), tm,
                 final)
    return h


def kernel(x_prompt, x_sample, meta_tokens, norm_mix, w_in, ret_decay, q_norm, k_norm, w_ret_o,
           w_att_o, w_out, norm_ffn, w_ffn_in, w_ffn_out, norm_final):
    y_prompt, y_sample = _trunk([x_prompt, x_sample], meta_tokens, norm_mix, w_in, ret_decay,
                                q_norm, k_norm, w_ret_o, w_att_o, w_out, norm_ffn, w_ffn_in,
                                w_ffn_out, norm_final)
    return (y_prompt, y_sample)
```

```python
import functools

import jax
import jax.numpy as jnp
from jax import lax
from jax.experimental import pallas as pl
from jax.experimental.pallas import tpu as pltpu

D_MODEL = 1024
DEPTH = 2
N_META = 16
GRID_W = 64
BLOCK = 128
PAD = BLOCK - N_META
RET_HEADS = 4
RET_QK_DIM = 128
RET_V_DIM = 256
ATT_HEADS = 8
ATT_KV_HEADS = 2
ATT_GROUP = ATT_HEADS // ATT_KV_HEADS
ATT_HEAD_DIM = 128
D_FF = ((8 * D_MODEL + 3 * 256 - 1) // (3 * 256)) * 256
ROPE_BASE = 10000.0
NORM_EPS = 1e-6
RET_QK_W = RET_HEADS * RET_QK_DIM
RET_V_W = RET_HEADS * RET_V_DIM
ATT_Q_W = ATT_HEADS * ATT_HEAD_DIM
ATT_KV_W = ATT_KV_HEADS * ATT_HEAD_DIM
ATT_GROUP_W = ATT_GROUP * ATT_HEAD_DIM
IN_NAMES = ("rq", "rk", "rv", "rg", "aq", "ak", "av", "gr", "ga")
IN_SPLITS = (RET_QK_W, RET_QK_W, RET_V_W, RET_V_W, ATT_Q_W, ATT_KV_W, ATT_KV_W, D_MODEL, D_MODEL)
IN_WIDTH = sum(IN_SPLITS)

LANES = 128
BF16_SUBLANES = 16
VMEM_BYTES_V7X = 64 * 1024 * 1024
VMEM_LIMIT = VMEM_BYTES_V7X * 7 // 8
ROW_TILE_TARGET = 688
ATT_Q_ROWS = 1056
ATT_SUB_ROWS = 352
FFN_CHUNK = 256
MASK_VALUE = -1e30

F32 = jnp.float32
BF16 = jnp.bfloat16

TAB_RC, TAB_RS, TAB_AC, TAB_ASP, TAB_ASM = range(5)
TAB_W = 5 * LANES


def _row_tile(rows, target=None):
    target = ROW_TILE_TARGET if target is None else target
    best = None
    for d in range(BF16_SUBLANES, min(rows, target) + 1, BF16_SUBLANES):
        if rows % d == 0:
            best = d
    assert best is not None, rows
    return best


def _params(n_axes):
    return pltpu.CompilerParams(dimension_semantics=("arbitrary",) * n_axes,
                                vmem_limit_bytes=VMEM_LIMIT)


def _resident(shape):
    return pl.BlockSpec(shape, lambda *_: (0,) * len(shape), pipeline_mode=pl.Buffered(1))


def _rms_rows(x):
    return x * lax.rsqrt(jnp.mean(x * x, axis=-1, keepdims=True) + NORM_EPS)


def _sigmoid(x):
    return 1.0 / (1.0 + jnp.exp(-x))


def _rotary_tables(s):
    r = jnp.arange(s + N_META, dtype=jnp.int32)
    is_tok = r < s
    t = jnp.where(is_tok, r + N_META, r - s)
    ret_inv = ROPE_BASE ** (-jnp.linspace(0.0, 1.0, RET_QK_DIM // 2, dtype=F32))
    ret_ang = t.astype(F32)[:, None] * ret_inv[None, :]
    rc, rs = jnp.cos(ret_ang), jnp.sin(ret_ang)
    row = jnp.where(is_tok, r // GRID_W, 0).astype(F32)
    col = jnp.where(is_tok, r % GRID_W, 0).astype(F32)
    ax_half = ATT_HEAD_DIM // 2
    ax_inv = ROPE_BASE ** (-jnp.arange(ax_half // 2, dtype=F32) * 2.0 / ax_half)
    ra, ca = row[:, None] * ax_inv[None, :], col[:, None] * ax_inv[None, :]
    z = jnp.zeros_like(ra)
    return jnp.concatenate([
        rc, rc,
        -rs, rs,
        jnp.cos(ra), jnp.cos(ra), jnp.cos(ca), jnp.cos(ca),
        -jnp.sin(ra), z, -jnp.sin(ca), z,
        z, jnp.sin(ra), z, jnp.sin(ca),
    ], axis=1)


class _Rows:
    def __init__(self, tm, nt, s, counts, first_layer):
        self.tm, self.nt, self.s, self.counts, self.first_layer = tm, nt, s, counts, first_layer

    def specs(self):
        tm, nt = self.tm, self.nt
        if not self.first_layer:
            return [pl.BlockSpec((1, tm, D_MODEL), lambda i, j: (i, j, 0))]
        n0 = self.counts[0]

        def first(i, j):
            mine = i < n0
            return (jnp.where(mine, i, n0 - 1), jnp.where(mine, j, nt - 1), 0)

        def second(i, j):
            mine = i >= n0
            return (jnp.where(mine, i - n0, 0), jnp.where(mine, j, 0), 0)

        return [pl.BlockSpec((1, tm, D_MODEL), first), pl.BlockSpec((1, tm, D_MODEL), second),
                _resident((N_META, D_MODEL))]

    def n_refs(self):
        return 3 if self.first_layer else 1

    def load(self, refs):
        if not self.first_layer:
            return refs[0][0]
        xa_ref, xb_ref, meta_ref = refs
        i, j = pl.program_id(0), pl.program_id(1)
        x = jnp.where(i < self.counts[0], xa_ref[0], xb_ref[0])
        meta_at = self.s - (self.nt - 1) * self.tm
        assert meta_at + N_META == self.tm
        meta_tile = jnp.concatenate([jnp.zeros((meta_at, D_MODEL), F32), meta_ref[...]], axis=0)
        row = j * self.tm + lax.broadcasted_iota(jnp.int32, (self.tm, 1), 0)
        return jnp.where(row < self.s, x, meta_tile)


def _row_inputs(rows, h, xs, meta):
    return [xs[0], xs[1], meta] if rows.first_layer else [h]


def _in_proj_kernel(rows, *refs):
    row_refs, refs = refs[:rows.n_refs()], refs[rows.n_refs():]
    gain_ref, w_ref, tab_ref, qg_ref, kg_ref = refs[:5]
    outs = dict(zip(IN_NAMES, refs[5:]))
    offsets = {}
    lo = 0
    for name, width in zip(IN_NAMES, IN_SPLITS):
        offsets[name] = (lo, width)
        lo += width

    def tab(i):
        return tab_ref[:, i * LANES:(i + 1) * LANES]

    def ret_rope(x, scale):
        y = x * tab(TAB_RC) + pltpu.roll(x, RET_QK_DIM // 2, 1) * tab(TAB_RS)
        return y if scale is None else y * scale

    def axial(x, gain, scale):
        y = _rms_rows(x) * gain
        y = (y * tab(TAB_AC) + pltpu.roll(y, LANES - 32, 1) * tab(TAB_ASP)
             + pltpu.roll(y, 32, 1) * tab(TAB_ASM))
        return y if scale is None else y * scale

    x = rows.load(row_refs)
    xg = (x * gain_ref[...]).astype(BF16)
    inv_rms = lax.rsqrt(jnp.mean(x * x, axis=-1, keepdims=True) + NORM_EPS)

    def proj(name):
        lo, width = offsets[name]
        return jnp.dot(xg, w_ref[:, lo:lo + width], preferred_element_type=F32) * inv_rms

    def heads(name, fn):
        y = proj(name)
        for hd in range(y.shape[1] // LANES):
            sl = slice(hd * LANES, (hd + 1) * LANES)
            outs[name][0, :, sl] = fn(y[:, sl]).astype(outs[name].dtype)

    heads("aq", lambda y: axial(y, qg_ref[...], ATT_HEAD_DIM ** -0.5))
    heads("ak", lambda y: axial(y, kg_ref[...], None))
    heads("rq", lambda y: ret_rope(y, None))
    heads("rk", lambda y: ret_rope(y, RET_QK_DIM ** -0.5))
    for name in ("av", "rv", "rg", "gr", "ga"):
        outs[name][0] = proj(name).astype(outs[name].dtype)


def _in_proj(rows, h, xs, meta, gain, w_in, tabs, q_gain, k_gain, b):
    tm, nt = rows.tm, rows.nt

    def out_rows(w):
        return pl.BlockSpec((1, tm, w), lambda i, j: (i, j, 0))

    return pl.pallas_call(
        functools.partial(_in_proj_kernel, rows),
        grid=(b, nt),
        in_specs=rows.specs() + [_resident((1, D_MODEL)), _resident((D_MODEL, IN_WIDTH)),
                                 pl.BlockSpec((tm, TAB_W), lambda i, j: (j, 0)),
                                 _resident((1, LANES)), _resident((1, LANES))],
        out_specs=[out_rows(w) for w in IN_SPLITS],
        out_shape=[jax.ShapeDtypeStruct((b, nt * tm, w), BF16) for w in IN_SPLITS],
        compiler_params=_params(2),
        name="in_proj",
    )(*_row_inputs(rows, h, xs, meta), gain, w_in, tabs, q_gain, k_gain)


(COEF_DECAY, COEF_Q_FWD, COEF_Q_BWD, COEF_K_FWD, COEF_K_BWD,
 COEF_Q_FWD_META, COEF_Q_BWD_META, COEF_K_FWD_META, COEF_K_BWD_META) = range(9)
N_COEF = 9
FWD, BWD = 0, 1


def _retention_kernel(lg_ref, q_ref, k_ref, v_ref, g_ref, o_ref, coef_ref, state_ref, stack_ref):
    s_rows = q_ref.shape[1] - N_META
    nc = s_rows // BLOCK + 1
    row = lax.broadcasted_iota(jnp.int32, (BLOCK, BLOCK), 0).astype(F32)
    col = lax.broadcasted_iota(jnp.int32, (BLOCK, BLOCK), 1).astype(F32)
    rel = row - col
    meta_pos = jnp.minimum(row + PAD, BLOCK - 1.0)
    chunk_decay = []
    for hd in range(RET_HEADS):
        log_f, log_b = lg_ref[FWD, hd], lg_ref[BWD, hd]
        coef_ref[hd, COEF_DECAY] = jnp.where(rel >= 0, jnp.exp(log_f * jnp.maximum(rel, 0.0)),
                                             jnp.exp(log_b * jnp.maximum(-rel, 0.0)))
        for pos, shift in ((row, 0), (meta_pos, COEF_Q_FWD_META - COEF_Q_FWD)):
            coef_ref[hd, COEF_Q_FWD + shift] = jnp.exp(log_f * (pos + 1.0))
            coef_ref[hd, COEF_Q_BWD + shift] = jnp.exp(log_b * (BLOCK - pos))
            coef_ref[hd, COEF_K_FWD + shift] = jnp.exp(log_f * (BLOCK - 1.0 - pos))
            coef_ref[hd, COEF_K_BWD + shift] = jnp.exp(log_b * pos)
        ones_row = jnp.ones((1, RET_V_DIM), F32)
        chunk_decay.append((jnp.exp(log_f * BLOCK * ones_row), jnp.exp(log_b * BLOCK * ones_row)))
    state_ref[...] = jnp.zeros_like(state_ref)

    def coef_shift(c):
        return COEF_Q_FWD_META - COEF_Q_FWD if c == 0 else 0

    def load(ref, c, cols):
        if c == 0:
            meta = ref[0, s_rows:s_rows + N_META, cols]
            return jnp.concatenate([meta, jnp.zeros((PAD, meta.shape[1]), meta.dtype)], axis=0)
        return ref[0, (c - 1) * BLOCK:c * BLOCK, cols]

    def qk_cols(hd):
        return slice(hd * RET_QK_DIM, (hd + 1) * RET_QK_DIM)

    def v_cols(hd):
        return slice(hd * RET_V_DIM, (hd + 1) * RET_V_DIM)

    def stack_rows(c, hd, direction=None):
        base = (c * RET_HEADS + hd) * 2 * BLOCK
        if direction is None:
            return slice(base, base + 2 * BLOCK)
        return slice(base + direction * BLOCK, base + (direction + 1) * BLOCK)

    def scan_step(c, direction, update):
        shift = coef_shift(c)
        for hd in range(RET_HEADS):
            slot = direction * RET_HEADS + hd
            state = state_ref[slot]
            stack_ref[stack_rows(c, hd, direction), :] = state.astype(stack_ref.dtype)
            if update:
                k = (load(k_ref, c, qk_cols(hd)).astype(F32)
                     * coef_ref[hd, COEF_K_FWD + direction + shift])
                pushed = lax.dot_general(k.astype(BF16), load(v_ref, c, v_cols(hd)),
                                         (((0,), (0,)), ((), ())), preferred_element_type=F32)
                state_ref[slot] = state * chunk_decay[hd][direction] + pushed

    for t in range(nc):
        scan_step(t, FWD, update=t < nc - 1)
        scan_step(nc - 1 - t, BWD, update=t < nc - 1)

    def mixed(c, hd):
        shift = coef_shift(c)
        q, k, v = load(q_ref, c, qk_cols(hd)), load(k_ref, c, qk_cols(hd)), load(v_ref, c, v_cols(hd))
        s = lax.dot_general(q, k, (((1,), (1,)), ((), ())), preferred_element_type=F32)
        qf = q.astype(F32)
        lhs = jnp.concatenate([(s * coef_ref[hd, COEF_DECAY]).astype(BF16),
                               (qf * coef_ref[hd, COEF_Q_FWD + shift]).astype(BF16),
                               (qf * coef_ref[hd, COEF_Q_BWD + shift]).astype(BF16)], axis=1)
        return lhs, jnp.concatenate([v, stack_ref[stack_rows(c, hd), :]], axis=0)

    items = [(c, hd) for c in range(nc) for hd in range(RET_HEADS)]
    nxt = mixed(*items[0])
    for idx, (c, hd) in enumerate(items):
        lhs, rhs = nxt
        if idx + 1 < len(items):
            nxt = mixed(*items[idx + 1])
        o = _rms_rows(jnp.dot(lhs, rhs, preferred_element_type=F32))
        g = load(g_ref, c, v_cols(hd)).astype(F32)
        o = (o * (g * _sigmoid(g))).astype(o_ref.dtype)
        if c == 0:
            o_ref[0, s_rows:s_rows + N_META, v_cols(hd)] = o[:N_META]
        else:
            o_ref[0, (c - 1) * BLOCK:c * BLOCK, v_cols(hd)] = o


def _retention(log_gamma, rq, rk, rv, rg):
    b, lh, _ = rq.shape
    nc = (lh - N_META) // BLOCK + 1

    def seq(w):
        return pl.BlockSpec((1, lh, w), lambda i, lg: (i, 0, 0))

    return pl.pallas_call(
        _retention_kernel,
        grid_spec=pltpu.PrefetchScalarGridSpec(
            num_scalar_prefetch=1,
            grid=(b,),
            in_specs=[seq(RET_QK_W), seq(RET_QK_W), seq(RET_V_W), seq(RET_V_W)],
            out_specs=seq(RET_V_W),
            scratch_shapes=[pltpu.VMEM((RET_HEADS, N_COEF, BLOCK, BLOCK), F32),
                            pltpu.VMEM((2 * RET_HEADS, RET_QK_DIM, RET_V_DIM), F32),
                            pltpu.VMEM((nc * RET_HEADS * 2 * BLOCK, RET_V_DIM), BF16)],
        ),
        out_shape=jax.ShapeDtypeStruct((b, lh, RET_V_W), BF16),
        compiler_params=_params(1),
        name="retention",
    )(log_gamma, rq, rk, rv, rg)


def _attention_kernel(q_ref, k_ref, v_ref, o_ref):
    tq = q_ref.shape[1]
    lh = k_ref.shape[1]
    lp = lh + PAD
    sub = _row_tile(tq, ATT_SUB_ROWS)
    key = lax.broadcasted_iota(jnp.int32, (1, lp), 1)
    zeros = jnp.zeros((PAD, ATT_HEAD_DIM), k_ref.dtype)
    k = jnp.concatenate([k_ref[0], zeros], axis=0)
    v1 = jnp.concatenate([jnp.concatenate([v_ref[0], zeros], axis=0),
                          jnp.ones((lp, LANES), v_ref.dtype)], axis=1)
    chains = [(slice(r0, r0 + sub), slice(g * LANES, (g + 1) * LANES))
              for r0 in range(0, tq, sub) for g in range(ATT_GROUP)]

    def scores(c):
        rows, cols = chains[c]
        s = lax.dot_general(q_ref[0, rows, cols], k, (((1,), (1,)), ((), ())),
                            preferred_element_type=F32)
        return jnp.where(key < lh, s, MASK_VALUE)

    s_next = scores(0)
    for c, (rows, cols) in enumerate(chains):
        s = s_next
        if c + 1 < len(chains):
            s_next = scores(c + 1)
        p = jnp.exp(s - jnp.max(s, axis=-1, keepdims=True))
        ov = jnp.dot(p.astype(BF16), v1, preferred_element_type=F32)
        o = ov[:, :ATT_HEAD_DIM] * (1.0 / ov[:, ATT_HEAD_DIM:])
        o_ref[0, rows, cols] = o.astype(o_ref.dtype)


def _attention(aq, ak, av):
    b, lh, _ = aq.shape
    tq = min(ATT_Q_ROWS, lh)
    return pl.pallas_call(
        _attention_kernel,
        grid=(b, ATT_KV_HEADS, pl.cdiv(lh, tq)),
        in_specs=[pl.BlockSpec((1, tq, ATT_GROUP_W), lambda i, j, t: (i, t, j)),
                  pl.BlockSpec((1, lh, ATT_HEAD_DIM), lambda i, j, t: (i, 0, j)),
                  pl.BlockSpec((1, lh, ATT_HEAD_DIM), lambda i, j, t: (i, 0, j))],
        out_specs=pl.BlockSpec((1, tq, ATT_GROUP_W), lambda i, j, t: (i, t, j)),
        out_shape=jax.ShapeDtypeStruct((b, lh, ATT_Q_W), BF16),
        compiler_params=_params(3),
        name="attention",
    )(aq, ak, av)


def _mix_out_kernel(rows, *refs):
    row_refs, refs = refs[:rows.n_refs()], refs[rows.n_refs():]
    ret_ref, att_ref, gr_ref, ga_ref, wr_ref, wa_ref, wo_ref, o_ref = refs
    r = jnp.dot(ret_ref[0], wr_ref[...], preferred_element_type=F32)
    a = jnp.dot(att_ref[0], wa_ref[...], preferred_element_type=F32)
    merged = _sigmoid(gr_ref[0].astype(F32)) * r + _sigmoid(ga_ref[0].astype(F32)) * a
    o_ref[0] = rows.load(row_refs) + jnp.dot(merged.astype(BF16), wo_ref[...],
                                             preferred_element_type=F32)


def _mix_out(rows, h, xs, meta, ret, att, gr, ga, w_ret_o, w_att_o, w_out, b):
    tm, nt = rows.tm, rows.nt
    tile = pl.BlockSpec((1, tm, D_MODEL), lambda i, j: (i, j, 0))
    w = _resident((D_MODEL, D_MODEL))
    return pl.pallas_call(
        functools.partial(_mix_out_kernel, rows),
        grid=(b, nt),
        in_specs=rows.specs() + [tile] * 4 + [w] * 3,
        out_specs=tile,
        out_shape=jax.ShapeDtypeStruct((b, nt * tm, D_MODEL), F32),
        input_output_aliases={} if rows.first_layer else {0: 0},
        compiler_params=_params(2),
        name="mix_out",
    )(*_row_inputs(rows, h, xs, meta), ret, att, gr, ga, w_ret_o, w_att_o, w_out)


def _ffn_kernel(final_counts, h_ref, gain_ref, wi_ref, wo_ref, *rest):
    x = h_ref[0]
    xg = (x * gain_ref[...]).astype(BF16)
    inv_rms = lax.rsqrt(jnp.mean(x * x, axis=-1, keepdims=True) + NORM_EPS)
    def pair(j):
        lo = j * FFN_CHUNK
        return (jnp.dot(xg, wi_ref[:, lo:lo + FFN_CHUNK], preferred_element_type=F32),
                jnp.dot(xg, wi_ref[:, D_FF + lo:D_FF + lo + FFN_CHUNK],
                        preferred_element_type=F32))

    acts = []
    nxt = pair(0)
    for j in range(D_FF // FFN_CHUNK):
        a, u = nxt
        if j + 1 < D_FF // FFN_CHUNK:
            nxt = pair(j + 1)
        a = a * inv_rms
        acts.append((a * _sigmoid(a) * (u * inv_rms)).astype(BF16))
    act = jnp.concatenate(acts, axis=1)
    out = x + jnp.dot(act, wo_ref[...], preferred_element_type=F32)
    if final_counts is None:
        (o_ref,) = rest
        o_ref[0] = out
        return
    fgain_ref, ya_ref, yb_ref = rest
    y = _rms_rows(out) * fgain_ref[...]

    @pl.when(pl.program_id(0) < final_counts[0])
    def _():
        ya_ref[0] = y

    @pl.when(pl.program_id(0) >= final_counts[0])
    def _():
        yb_ref[0] = y


def _ffn(h, gain, w_ffn_in, w_ffn_out, tm, final=None):
    b, lh, _ = h.shape
    nt = lh // tm
    tile = pl.BlockSpec((1, tm, D_MODEL), lambda i, j: (i, j, 0))
    in_specs = [tile, _resident((1, D_MODEL)), _resident((D_MODEL, 2 * D_FF)),
                _resident((D_FF, D_MODEL))]
    if final is None:
        return pl.pallas_call(
            functools.partial(_ffn_kernel, None),
            grid=(b, nt),
            in_specs=in_specs,
            out_specs=tile,
            out_shape=jax.ShapeDtypeStruct(h.shape, F32),
            input_output_aliases={0: 0},
            compiler_params=_params(2),
            name="ffn",
        )(h, gain, w_ffn_in, w_ffn_out)
    fgain, counts, s = final
    n0 = counts[0]

    def first(i, j):
        mine = i < n0
        return (jnp.where(mine, i, n0 - 1), jnp.where(mine, j, nt - 1), 0)

    def second(i, j):
        mine = i >= n0
        return (jnp.where(mine, i - n0, 0), jnp.where(mine, j, 0), 0)

    return pl.pallas_call(
        functools.partial(_ffn_kernel, counts),
        grid=(b, nt),
        in_specs=in_specs + [_resident((1, D_MODEL))],
        out_specs=[pl.BlockSpec((1, tm, D_MODEL), first), pl.BlockSpec((1, tm, D_MODEL), second)],
        out_shape=[jax.ShapeDtypeStruct((c, s, D_MODEL), F32) for c in counts],
        compiler_params=_params(2),
        name="ffn_final",
    )(h, gain, w_ffn_in, w_ffn_out, fgain)


def _trunk(xs, meta_tokens, norm_mix, w_in, ret_decay, q_norm, k_norm, w_ret_o, w_att_o, w_out,
           norm_ffn, w_ffn_in, w_ffn_out, norm_final):
    assert len(xs) == 2
    s = xs[0].shape[1]
    assert all(x.shape[1:] == (s, D_MODEL) for x in xs) and s % GRID_W == 0 and s % BLOCK == 0
    lh = s + N_META
    counts = tuple(x.shape[0] for x in xs)
    b = sum(counts)
    tm = _row_tile(lh)
    nt = lh // tm
    assert s > (nt - 1) * tm, "the meta tokens must sit in the last row tile"
    meta = meta_tokens.astype(F32)
    tabs = _rotary_tables(s)
    gamma = 1.0 - jnp.exp2(-ret_decay.astype(F32))
    log_gamma = jnp.log(gamma)
    h = None
    for l in range(DEPTH):
        rows = _Rows(tm, nt, s, counts, first_layer=l == 0)
        rq, rk, rv, rg, aq, ak, av, gr, ga = _in_proj(
            rows, h, xs, meta, norm_mix[l][None], w_in[l].astype(BF16), tabs, q_norm[l][None],
            k_norm[l][None], b)
        ret = _retention(log_gamma[l], rq, rk, rv, rg)
        att = _attention(aq, ak, av)
        h = _mix_out(rows, h, xs, meta, ret, att, gr, ga, w_ret_o[l].astype(BF16),
                     w_att_o[l].astype(BF16), w_out[l].astype(BF16), b)
        final = (norm_final[None], counts, s) if l == DEPTH - 1 else None
        h = _ffn(h, norm_ffn[l][None], w_ffn_in[l].astype(BF16), w_ffn_out[l].astype(BF16), tm,
                 final)
    return h


def kernel(x_prompt, x_sample, meta_tokens, norm_mix, w_in, ret_decay, q_norm, k_norm, w_ret_o,
           w_att_o, w_out, norm_ffn, w_ffn_in, w_ffn_out, norm_final):
    y_prompt, y_sample = _trunk([x_prompt, x_sample], meta_tokens, norm_mix, w_in, ret_decay,
                                q_norm, k_norm, w_ret_o, w_att_o, w_out, norm_ffn, w_ffn_in,
                                w_ffn_out, norm_final)
    return (y_prompt, y_sample)
```

```python
import functools

import jax
import jax.numpy as jnp
from jax import lax
from jax.experimental import pallas as pl
from jax.experimental.pallas import tpu as pltpu

D_MODEL = 1024
DEPTH = 2
N_META = 16
GRID_W = 64
BLOCK = 128
PAD = BLOCK - N_META
RET_HEADS = 4
RET_QK_DIM = 128
RET_V_DIM = 256
ATT_HEADS = 8
ATT_KV_HEADS = 2
ATT_GROUP = ATT_HEADS // ATT_KV_HEADS
ATT_HEAD_DIM = 128
D_FF = ((8 * D_MODEL + 3 * 256 - 1) // (3 * 256)) * 256
ROPE_BASE = 10000.0
NORM_EPS = 1e-6
RET_QK_W = RET_HEADS * RET_QK_DIM
RET_V_W = RET_HEADS * RET_V_DIM
ATT_Q_W = ATT_HEADS * ATT_HEAD_DIM
ATT_KV_W = ATT_KV_HEADS * ATT_HEAD_DIM
ATT_GROUP_W = ATT_GROUP * ATT_HEAD_DIM
IN_NAMES = ("rq", "rk", "rv", "rg", "aq", "ak", "av", "gr", "ga")
IN_SPLITS = (RET_QK_W, RET_QK_W, RET_V_W, RET_V_W, ATT_Q_W, ATT_KV_W, ATT_KV_W, D_MODEL, D_MODEL)
IN_WIDTH = sum(IN_SPLITS)

LANES = 128
BF16_SUBLANES = 16
VMEM_BYTES_V7X = 64 * 1024 * 1024
VMEM_LIMIT = VMEM_BYTES_V7X * 7 // 8
ROW_TILE_TARGET = 688
ATT_Q_ROWS = 1056
ATT_SUB_ROWS = 352
FFN_CHUNK = 256
MASK_VALUE = -1e30

F32 = jnp.float32
BF16 = jnp.bfloat16

TAB_RC, TAB_RS, TAB_AC, TAB_ASP, TAB_ASM = range(5)
TAB_W = 5 * LANES


def _row_tile(rows, target=None):
    target = ROW_TILE_TARGET if target is None else target
    best = None
    for d in range(BF16_SUBLANES, min(rows, target) + 1, BF16_SUBLANES):
        if rows % d == 0:
            best = d
    assert best is not None, rows
    return best


def _params(n_axes):
    return pltpu.CompilerParams(dimension_semantics=("arbitrary",) * n_axes,
                                vmem_limit_bytes=VMEM_LIMIT)


def _resident(shape):
    return pl.BlockSpec(shape, lambda *_: (0,) * len(shape), pipeline_mode=pl.Buffered(1))


def _rms_rows(x):
    return x * lax.rsqrt(jnp.mean(x * x, axis=-1, keepdims=True) + NORM_EPS)


def _sigmoid(x):
    return 1.0 / (1.0 + jnp.exp(-x))


def _rotary_tables(s):
    r = jnp.arange(s + N_META, dtype=jnp.int32)
    is_tok = r < s
    t = jnp.where(is_tok, r + N_META, r - s)
    ret_inv = ROPE_BASE ** (-jnp.linspace(0.0, 1.0, RET_QK_DIM // 2, dtype=F32))
    ret_ang = t.astype(F32)[:, None] * ret_inv[None, :]
    rc, rs = jnp.cos(ret_ang), jnp.sin(ret_ang)
    row = jnp.where(is_tok, r // GRID_W, 0).astype(F32)
    col = jnp.where(is_tok, r % GRID_W, 0).astype(F32)
    ax_half = ATT_HEAD_DIM // 2
    ax_inv = ROPE_BASE ** (-jnp.arange(ax_half // 2, dtype=F32) * 2.0 / ax_half)
    ra, ca = row[:, None] * ax_inv[None, :], col[:, None] * ax_inv[None, :]
    z = jnp.zeros_like(ra)
    return jnp.concatenate([
        rc, rc,
        -rs, rs,
        jnp.cos(ra), jnp.cos(ra), jnp.cos(ca), jnp.cos(ca),
        -jnp.sin(ra), z, -jnp.sin(ca), z,
        z, jnp.sin(ra), z, jnp.sin(ca),
    ], axis=1)


class _Rows:
    def __init__(self, tm, nt, s, counts, first_layer):
        self.tm, self.nt, self.s, self.counts, self.first_layer = tm, nt, s, counts, first_layer

    def specs(self):
        tm, nt = self.tm, self.nt
        if not self.first_layer:
            return [pl.BlockSpec((1, tm, D_MODEL), lambda i, j: (i, j, 0))]
        n0 = self.counts[0]

        def first(i, j):
            mine = i < n0
            return (jnp.where(mine, i, n0 - 1), jnp.where(mine, j, nt - 1), 0)

        def second(i, j):
            mine = i >= n0
            return (jnp.where(mine, i - n0, 0), jnp.where(mine, j, 0), 0)

        return [pl.BlockSpec((1, tm, D_MODEL), first), pl.BlockSpec((1, tm, D_MODEL), second),
                _resident((N_META, D_MODEL))]

    def n_refs(self):
        return 3 if self.first_layer else 1

    def load(self, refs):
        if not self.first_layer:
            return refs[0][0]
        xa_ref, xb_ref, meta_ref = refs
        i, j = pl.program_id(0), pl.program_id(1)
        x = jnp.where(i < self.counts[0], xa_ref[0], xb_ref[0])
        meta_at = self.s - (self.nt - 1) * self.tm
        assert meta_at + N_META == self.tm
        meta_tile = jnp.concatenate([jnp.zeros((meta_at, D_MODEL), F32), meta_ref[...]], axis=0)
        row = j * self.tm + lax.broadcasted_iota(jnp.int32, (self.tm, 1), 0)
        return jnp.where(row < self.s, x, meta_tile)


def _row_inputs(rows, h, xs, meta):
    return [xs[0], xs[1], meta] if rows.first_layer else [h]


def _in_proj_kernel(rows, *refs):
    row_refs, refs = refs[:rows.n_refs()], refs[rows.n_refs():]
    gain_ref, w_ref, tab_ref, qg_ref, kg_ref = refs[:5]
    outs = dict(zip(IN_NAMES, refs[5:]))
    offsets = {}
    lo = 0
    for name, width in zip(IN_NAMES, IN_SPLITS):
        offsets[name] = (lo, width)
        lo += width

    def tab(i):
        return tab_ref[:, i * LANES:(i + 1) * LANES]

    def ret_rope(x, scale):
        y = x * tab(TAB_RC) + pltpu.roll(x, RET_QK_DIM // 2, 1) * tab(TAB_RS)
        return y if scale is None else y * scale

    def axial(x, gain, scale):
        y = _rms_rows(x) * gain
        y = (y * tab(TAB_AC) + pltpu.roll(y, LANES - 32, 1) * tab(TAB_ASP)
             + pltpu.roll(y, 32, 1) * tab(TAB_ASM))
        return y if scale is None else y * scale

    x = rows.load(row_refs)
    xg = (x * gain_ref[...]).astype(BF16)
    inv_rms = lax.rsqrt(jnp.mean(x * x, axis=-1, keepdims=True) + NORM_EPS)

    def proj(name):
        lo, width = offsets[name]
        return jnp.dot(xg, w_ref[:, lo:lo + width], preferred_element_type=F32) * inv_rms

    def heads(name, fn):
        y = proj(name)
        for hd in range(y.shape[1] // LANES):
            sl = slice(hd * LANES, (hd + 1) * LANES)
            outs[name][0, :, sl] = fn(y[:, sl]).astype(outs[name].dtype)

    heads("aq", lambda y: axial(y, qg_ref[...], ATT_HEAD_DIM ** -0.5))
    heads("ak", lambda y: axial(y, kg_ref[...], None))
    heads("rq", lambda y: ret_rope(y, None))
    heads("rk", lambda y: ret_rope(y, RET_QK_DIM ** -0.5))
    for name in ("av", "rv", "rg", "gr", "ga"):
        outs[name][0] = proj(name).astype(outs[name].dtype)


def _in_proj(rows, h, xs, meta, gain, w_in, tabs, q_gain, k_gain, b):
    tm, nt = rows.tm, rows.nt

    def out_rows(w):
        return pl.BlockSpec((1, tm, w), lambda i, j: (i, j, 0))

    return pl.pallas_call(
        functools.partial(_in_proj_kernel, rows),
        grid=(b, nt),
        in_specs=rows.specs() + [_resident((1, D_MODEL)), _resident((D_MODEL, IN_WIDTH)),
                                 pl.BlockSpec((tm, TAB_W), lambda i, j: (j, 0)),
                                 _resident((1, LANES)), _resident((1, LANES))],
        out_specs=[out_rows(w) for w in IN_SPLITS],
        out_shape=[jax.ShapeDtypeStruct((b, nt * tm, w), BF16) for w in IN_SPLITS],
        compiler_params=_params(2),
        name="in_proj",
    )(*_row_inputs(rows, h, xs, meta), gain, w_in, tabs, q_gain, k_gain)


(COEF_DECAY, COEF_Q_FWD, COEF_Q_BWD, COEF_K_FWD, COEF_K_BWD,
 COEF_Q_FWD_META, COEF_Q_BWD_META, COEF_K_FWD_META, COEF_K_BWD_META) = range(9)
N_COEF = 9
FWD, BWD = 0, 1


def _retention_kernel(lg_ref, q_ref, k_ref, v_ref, g_ref, o_ref, coef_ref, state_ref, stack_ref):
    s_rows = q_ref.shape[1] - N_META
    nc = s_rows // BLOCK + 1
    row = lax.broadcasted_iota(jnp.int32, (BLOCK, BLOCK), 0).astype(F32)
    col = lax.broadcasted_iota(jnp.int32, (BLOCK, BLOCK), 1).astype(F32)
    rel = row - col
    meta_pos = jnp.minimum(row + PAD, BLOCK - 1.0)
    chunk_decay = []
    for hd in range(RET_HEADS):
        log_f, log_b = lg_ref[FWD, hd], lg_ref[BWD, hd]
        coef_ref[hd, COEF_DECAY] = jnp.where(rel >= 0, jnp.exp(log_f * jnp.maximum(rel, 0.0)),
                                             jnp.exp(log_b * jnp.maximum(-rel, 0.0)))
        for pos, shift in ((row, 0), (meta_pos, COEF_Q_FWD_META - COEF_Q_FWD)):
            coef_ref[hd, COEF_Q_FWD + shift] = jnp.exp(log_f * (pos + 1.0))
            coef_ref[hd, COEF_Q_BWD + shift] = jnp.exp(log_b * (BLOCK - pos))
            coef_ref[hd, COEF_K_FWD + shift] = jnp.exp(log_f * (BLOCK - 1.0 - pos))
            coef_ref[hd, COEF_K_BWD + shift] = jnp.exp(log_b * pos)
        ones_row = jnp.ones((1, RET_V_DIM), F32)
        chunk_decay.append((jnp.exp(log_f * BLOCK * ones_row), jnp.exp(log_b * BLOCK * ones_row)))
    state_ref[...] = jnp.zeros_like(state_ref)

    def coef_shift(c):
        return COEF_Q_FWD_META - COEF_Q_FWD if c == 0 else 0

    def load(ref, c, cols):
        if c == 0:
            meta = ref[0, s_rows:s_rows + N_META, cols]
            return jnp.concatenate([meta, jnp.zeros((PAD, meta.shape[1]), meta.dtype)], axis=0)
        return ref[0, (c - 1) * BLOCK:c * BLOCK, cols]

    def qk_cols(hd):
        return slice(hd * RET_QK_DIM, (hd + 1) * RET_QK_DIM)

    def v_cols(hd):
        return slice(hd * RET_V_DIM, (hd + 1) * RET_V_DIM)

    def stack_rows(c, hd, direction=None):
        base = (c * RET_HEADS + hd) * 2 * BLOCK
        if direction is None:
            return slice(base, base + 2 * BLOCK)
        return slice(base + direction * BLOCK, base + (direction + 1) * BLOCK)

    def scan_step(c, direction, update):
        shift = coef_shift(c)
        for hd in range(RET_HEADS):
            slot = direction * RET_HEADS + hd
            state = state_ref[slot]
            stack_ref[stack_rows(c, hd, direction), :] = state.astype(stack_ref.dtype)
            if update:
                k = (load(k_ref, c, qk_cols(hd)).astype(F32)
                     * coef_ref[hd, COEF_K_FWD + direction + shift])
                pushed = lax.dot_general(k.astype(BF16), load(v_ref, c, v_cols(hd)),
                                         (((0,), (0,)), ((), ())), preferred_element_type=F32)
                state_ref[slot] = state * chunk_decay[hd][direction] + pushed

    for t in range(nc):
        scan_step(t, FWD, update=t < nc - 1)
        scan_step(nc - 1 - t, BWD, update=t < nc - 1)

    def mixed(c, hd):
        shift = coef_shift(c)
        q, k, v = load(q_ref, c, qk_cols(hd)), load(k_ref, c, qk_cols(hd)), load(v_ref, c, v_cols(hd))
        s = lax.dot_general(q, k, (((1,), (1,)), ((), ())), preferred_element_type=F32)
        qf = q.astype(F32)
        lhs = jnp.concatenate([(s * coef_ref[hd, COEF_DECAY]).astype(BF16),
                               (qf * coef_ref[hd, COEF_Q_FWD + shift]).astype(BF16),
                               (qf * coef_ref[hd, COEF_Q_BWD + shift]).astype(BF16)], axis=1)
        return lhs, jnp.concatenate([v, stack_ref[stack_rows(c, hd), :]], axis=0)

    items = [(c, hd) for c in range(nc) for hd in range(RET_HEADS)]
    nxt = mixed(*items[0])
    for idx, (c, hd) in enumerate(items):
        lhs, rhs = nxt
        if idx + 1 < len(items):
            nxt = mixed(*items[idx + 1])
        o = _rms_rows(jnp.dot(lhs, rhs, preferred_element_type=F32))
        g = load(g_ref, c, v_cols(hd)).astype(F32)
        o = (o * (g * _sigmoid(g))).astype(o_ref.dtype)
        if c == 0:
            o_ref[0, s_rows:s_rows + N_META, v_cols(hd)] = o[:N_META]
        else:
            o_ref[0, (c - 1) * BLOCK:c * BLOCK, v_cols(hd)] = o


def _retention(log_gamma, rq, rk, rv, rg):
    b, lh, _ = rq.shape
    nc = (lh - N_META) // BLOCK + 1

    def seq(w):
        return pl.BlockSpec((1, lh, w), lambda i, lg: (i, 0, 0))

    return pl.pallas_call(
        _retention_kernel,
        grid_spec=pltpu.PrefetchScalarGridSpec(
            num_scalar_prefetch=1,
            grid=(b,),
            in_specs=[seq(RET_QK_W), seq(RET_QK_W), seq(RET_V_W), seq(RET_V_W)],
            out_specs=seq(RET_V_W),
            scratch_shapes=[pltpu.VMEM((RET_HEADS, N_COEF, BLOCK, BLOCK), F32),
                            pltpu.VMEM((2 * RET_HEADS, RET_QK_DIM, RET_V_DIM), F32),
                            pltpu.VMEM((nc * RET_HEADS * 2 * BLOCK, RET_V_DIM), BF16)],
        ),
        out_shape=jax.ShapeDtypeStruct((b, lh, RET_V_W), BF16),
        compiler_params=_params(1),
        name="retention",
    )(log_gamma, rq, rk, rv, rg)


def _attention_kernel(q_ref, k_ref, v_ref, o_ref):
    tq = q_ref.shape[1]
    lh = k_ref.shape[1]
    lp = lh + PAD
    sub = _row_tile(tq, ATT_SUB_ROWS)
    key = lax.broadcasted_iota(jnp.int32, (1, lp), 1)
    zeros = jnp.zeros((PAD, ATT_HEAD_DIM), k_ref.dtype)
    k = jnp.concatenate([k_ref[0], zeros], axis=0)
    v1 = jnp.concatenate([jnp.concatenate([v_ref[0], zeros], axis=0),
                          jnp.ones((lp, LANES), v_ref.dtype)], axis=1)
    chains = [(slice(r0, r0 + sub), slice(g * LANES, (g + 1) * LANES))
              for r0 in range(0, tq, sub) for g in range(ATT_GROUP)]

    def scores(c):
        rows, cols = chains[c]
        s = lax.dot_general(q_ref[0, rows, cols], k, (((1,), (1,)), ((), ())),
                            preferred_element_type=F32)
        return jnp.where(key < lh, s, MASK_VALUE)

    s_next = scores(0)
    for c, (rows, cols) in enumerate(chains):
        s = s_next
        if c + 1 < len(chains):
            s_next = scores(c + 1)
        p = jnp.exp(s - jnp.max(s, axis=-1, keepdims=True))
        ov = jnp.dot(p.astype(BF16), v1, preferred_element_type=F32)
        o = ov[:, :ATT_HEAD_DIM] * (1.0 / ov[:, ATT_HEAD_DIM:])
        o_ref[0, rows, cols] = o.astype(o_ref.dtype)


def _attention(aq, ak, av):
    b, lh, _ = aq.shape
    tq = min(ATT_Q_ROWS, lh)
    return pl.pallas_call(
        _attention_kernel,
        grid=(b, ATT_KV_HEADS, pl.cdiv(lh, tq)),
        in_specs=[pl.BlockSpec((1, tq, ATT_GROUP_W), lambda i, j, t: (i, t, j)),
                  pl.BlockSpec((1, lh, ATT_HEAD_DIM), lambda i, j, t: (i, 0, j)),
                  pl.BlockSpec((1, lh, ATT_HEAD_DIM), lambda i, j, t: (i, 0, j))],
        out_specs=pl.BlockSpec((1, tq, ATT_GROUP_W), lambda i, j, t: (i, t, j)),
        out_shape=jax.ShapeDtypeStruct((b, lh, ATT_Q_W), BF16),
        compiler_params=_params(3),
        name="attention",
    )(aq, ak, av)


def _mix_out_kernel(rows, *refs):
    row_refs, refs = refs[:rows.n_refs()], refs[rows.n_refs():]
    ret_ref, att_ref, gr_ref, ga_ref, wr_ref, wa_ref, wo_ref, o_ref = refs
    r = jnp.dot(ret_ref[0], wr_ref[...], preferred_element_type=F32)
    a = jnp.dot(att_ref[0], wa_ref[...], preferred_element_type=F32)
    merged = _sigmoid(gr_ref[0].astype(F32)) * r + _sigmoid(ga_ref[0].astype(F32)) * a
    o_ref[0] = rows.load(row_refs) + jnp.dot(merged.astype(BF16), wo_ref[...],
                                             preferred_element_type=F32)


def _mix_out(rows, h, xs, meta, ret, att, gr, ga, w_ret_o, w_att_o, w_out, b):
    tm, nt = rows.tm, rows.nt
    tile = pl.BlockSpec((1, tm, D_MODEL), lambda i, j: (i, j, 0))
    w = _resident((D_MODEL, D_MODEL))
    return pl.pallas_call(
        functools.partial(_mix_out_kernel, rows),
        grid=(b, nt),
        in_specs=rows.specs() + [tile] * 4 + [w] * 3,
        out_specs=tile,
        out_shape=jax.ShapeDtypeStruct((b, nt * tm, D_MODEL), F32),
        input_output_aliases={} if rows.first_layer else {0: 0},
        compiler_params=_params(2),
        name="mix_out",
    )(*_row_inputs(rows, h, xs, meta), ret, att, gr, ga, w_ret_o, w_att_o, w_out)


def _ffn_kernel(is_final, h_ref, gain_ref, wi_ref, wo_ref, *rest):
    x = h_ref[0]
    xg = (x * gain_ref[...]).astype(BF16)
    inv_rms = lax.rsqrt(jnp.mean(x * x, axis=-1, keepdims=True) + NORM_EPS)
    def pair(j):
        lo = j * FFN_CHUNK
        return (jnp.dot(xg, wi_ref[:, lo:lo + FFN_CHUNK], preferred_element_type=F32),
                jnp.dot(xg, wi_ref[:, D_FF + lo:D_FF + lo + FFN_CHUNK],
                        preferred_element_type=F32))

    acts = []
    nxt = pair(0)
    for j in range(D_FF // FFN_CHUNK):
        a, u = nxt
        if j + 1 < D_FF // FFN_CHUNK:
            nxt = pair(j + 1)
        a = a * inv_rms
        acts.append((a * _sigmoid(a) * (u * inv_rms)).astype(BF16))
    act = jnp.concatenate(acts, axis=1)
    out = x + jnp.dot(act, wo_ref[...], preferred_element_type=F32)
    if not is_final:
        (o_ref,) = rest
        o_ref[0] = out
    else:
        fgain_ref, y_ref = rest
        y_ref[0] = _rms_rows(out) * fgain_ref[...]


def _ffn(h, gain, w_ffn_in, w_ffn_out, tm, final=None):
    b, lh, _ = h.shape
    nt = lh // tm
    tile = pl.BlockSpec((1, tm, D_MODEL), lambda i, j: (i, j, 0))
    weights = [_resident((1, D_MODEL)), _resident((D_MODEL, 2 * D_FF)), _resident((D_FF, D_MODEL))]
    if final is None:
        return pl.pallas_call(
            functools.partial(_ffn_kernel, False),
            grid=(b, nt),
            in_specs=[tile] + weights,
            out_specs=tile,
            out_shape=jax.ShapeDtypeStruct(h.shape, F32),
            input_output_aliases={0: 0},
            compiler_params=_params(2),
            name="ffn",
        )(h, gain, w_ffn_in, w_ffn_out)
    fgain, counts, s = final
    outs, first = [], 0
    for count in counts:
        outs.append(pl.pallas_call(
            functools.partial(_ffn_kernel, True),
            grid=(count, nt),
            in_specs=[pl.BlockSpec((1, tm, D_MODEL), lambda i, j, first=first: (i + first, j, 0))]
            + weights + [_resident((1, D_MODEL))],
            out_specs=tile,
            out_shape=jax.ShapeDtypeStruct((count, s, D_MODEL), F32),
            compiler_params=_params(2),
            name="ffn_final",
        )(h, gain, w_ffn_in, w_ffn_out, fgain))
        first += count
    return outs


def _trunk(xs, meta_tokens, norm_mix, w_in, ret_decay, q_norm, k_norm, w_ret_o, w_att_o, w_out,
           norm_ffn, w_ffn_in, w_ffn_out, norm_final):
    assert len(xs) == 2
    s = xs[0].shape[1]
    assert all(x.shape[1:] == (s, D_MODEL) for x in xs) and s % GRID_W == 0 and s % BLOCK == 0
    lh = s + N_META
    counts = tuple(x.shape[0] for x in xs)
    b = sum(counts)
    tm = _row_tile(lh)
    nt = lh // tm
    assert s > (nt - 1) * tm, "the meta tokens must sit in the last row tile"
    meta = meta_tokens.astype(F32)
    tabs = _rotary_tables(s)
    gamma = 1.0 - jnp.exp2(-ret_decay.astype(F32))
    log_gamma = jnp.log(gamma)
    h = None
    for l in range(DEPTH):
        rows = _Rows(tm, nt, s, counts, first_layer=l == 0)
        rq, rk, rv, rg, aq, ak, av, gr, ga = _in_proj(
            rows, h, xs, meta, norm_mix[l][None], w_in[l].astype(BF16), tabs, q_norm[l][None],
            k_norm[l][None], b)
        ret = _retention(log_gamma[l], rq, rk, rv, rg)
        att = _attention(aq, ak, av)
        h = _mix_out(rows, h, xs, meta, ret, att, gr, ga, w_ret_o[l].astype(BF16),
                     w_att_o[l].astype(BF16), w_out[l].astype(BF16), b)
        final = (norm_final[None], counts, s) if l == DEPTH - 1 else None
        h = _ffn(h, norm_ffn[l][None], w_ffn_in[l].astype(BF16), w_ffn_out[l].astype(BF16), tm,
                 final)
    return h


def kernel(x_prompt, x_sample, meta_tokens, norm_mix, w_in, ret_decay, q_norm, k_norm, w_ret_o,
           w_att_o, w_out, norm_ffn, w_ffn_in, w_ffn_out, norm_final):
    y_prompt, y_sample = _trunk([x_prompt, x_sample], meta_tokens, norm_mix, w_in, ret_decay,
                                q_norm, k_norm, w_ret_o, w_att_o, w_out, norm_ffn, w_ffn_in,
                                w_ffn_out, norm_final)
    return (y_prompt, y_sample)
```

```python
import functools

import jax
import jax.numpy as jnp
from jax import lax
from jax.experimental import pallas as pl
from jax.experimental.pallas import tpu as pltpu

D_MODEL = 1024
DEPTH = 2
N_META = 16
GRID_W = 64
BLOCK = 128
PAD = BLOCK - N_META
RET_HEADS = 4
RET_QK_DIM = 128
RET_V_DIM = 256
ATT_HEADS = 8
ATT_KV_HEADS = 2
ATT_GROUP = ATT_HEADS // ATT_KV_HEADS
ATT_HEAD_DIM = 128
D_FF = ((8 * D_MODEL + 3 * 256 - 1) // (3 * 256)) * 256
ROPE_BASE = 10000.0
NORM_EPS = 1e-6
RET_QK_W = RET_HEADS * RET_QK_DIM
RET_V_W = RET_HEADS * RET_V_DIM
ATT_Q_W = ATT_HEADS * ATT_HEAD_DIM
ATT_KV_W = ATT_KV_HEADS * ATT_HEAD_DIM
ATT_GROUP_W = ATT_GROUP * ATT_HEAD_DIM
IN_NAMES = ("rq", "rk", "rv", "rg", "aq", "ak", "av", "gr", "ga")
IN_SPLITS = (RET_QK_W, RET_QK_W, RET_V_W, RET_V_W, ATT_Q_W, ATT_KV_W, ATT_KV_W, D_MODEL, D_MODEL)
IN_WIDTH = sum(IN_SPLITS)

LANES = 128
BF16_SUBLANES = 16
VMEM_BYTES_V7X = 64 * 1024 * 1024
VMEM_LIMIT = VMEM_BYTES_V7X * 7 // 8
ROW_TILE_TARGET = 688
ATT_Q_ROWS = 1056
ATT_SUB_ROWS = 352
FFN_CHUNK = 256
MASK_VALUE = -1e30
ATT_Q_SCALE = ATT_HEAD_DIM ** -0.5 * 1.4426950408889634

F32 = jnp.float32
BF16 = jnp.bfloat16

TAB_RC, TAB_RS, TAB_AC, TAB_ASP, TAB_ASM = range(5)
TAB_W = 5 * LANES


def _row_tile(rows, target=None):
    target = ROW_TILE_TARGET if target is None else target
    best = None
    for d in range(BF16_SUBLANES, min(rows, target) + 1, BF16_SUBLANES):
        if rows % d == 0:
            best = d
    assert best is not None, rows
    return best


def _params(n_axes):
    return pltpu.CompilerParams(dimension_semantics=("arbitrary",) * n_axes,
                                vmem_limit_bytes=VMEM_LIMIT)


def _resident(shape):
    return pl.BlockSpec(shape, lambda *_: (0,) * len(shape), pipeline_mode=pl.Buffered(1))


def _rms_rows(x):
    return x * lax.rsqrt(jnp.mean(x * x, axis=-1, keepdims=True) + NORM_EPS)


def _sigmoid(x):
    return 1.0 / (1.0 + jnp.exp(-x))


def _rotary_tables(s):
    r = jnp.arange(s + N_META, dtype=jnp.int32)
    is_tok = r < s
    t = jnp.where(is_tok, r + N_META, r - s)
    ret_inv = ROPE_BASE ** (-jnp.linspace(0.0, 1.0, RET_QK_DIM // 2, dtype=F32))
    ret_ang = t.astype(F32)[:, None] * ret_inv[None, :]
    rc, rs = jnp.cos(ret_ang), jnp.sin(ret_ang)
    row = jnp.where(is_tok, r // GRID_W, 0).astype(F32)
    col = jnp.where(is_tok, r % GRID_W, 0).astype(F32)
    ax_half = ATT_HEAD_DIM // 2
    ax_inv = ROPE_BASE ** (-jnp.arange(ax_half // 2, dtype=F32) * 2.0 / ax_half)
    ra, ca = row[:, None] * ax_inv[None, :], col[:, None] * ax_inv[None, :]
    z = jnp.zeros_like(ra)
    return jnp.concatenate([
        rc, rc,
        -rs, rs,
        jnp.cos(ra), jnp.cos(ra), jnp.cos(ca), jnp.cos(ca),
        -jnp.sin(ra), z, -jnp.sin(ca), z,
        z, jnp.sin(ra), z, jnp.sin(ca),
    ], axis=1)


class _Rows:
    def __init__(self, tm, nt, s, counts, first_layer):
        self.tm, self.nt, self.s, self.counts, self.first_layer = tm, nt, s, counts, first_layer

    def specs(self):
        tm, nt = self.tm, self.nt
        if not self.first_layer:
            return [pl.BlockSpec((1, tm, D_MODEL), lambda i, j: (i, j, 0))]
        n0 = self.counts[0]

        def first(i, j):
            mine = i < n0
            return (jnp.where(mine, i, n0 - 1), jnp.where(mine, j, nt - 1), 0)

        def second(i, j):
            mine = i >= n0
            return (jnp.where(mine, i - n0, 0), jnp.where(mine, j, 0), 0)

        return [pl.BlockSpec((1, tm, D_MODEL), first), pl.BlockSpec((1, tm, D_MODEL), second),
                _resident((N_META, D_MODEL))]

    def n_refs(self):
        return 3 if self.first_layer else 1

    def load(self, refs):
        if not self.first_layer:
            return refs[0][0]
        xa_ref, xb_ref, meta_ref = refs
        i, j = pl.program_id(0), pl.program_id(1)
        x = jnp.where(i < self.counts[0], xa_ref[0], xb_ref[0])
        meta_at = self.s - (self.nt - 1) * self.tm
        assert meta_at + N_META == self.tm
        meta_tile = jnp.concatenate([jnp.zeros((meta_at, D_MODEL), F32), meta_ref[...]], axis=0)
        row = j * self.tm + lax.broadcasted_iota(jnp.int32, (self.tm, 1), 0)
        return jnp.where(row < self.s, x, meta_tile)


def _row_inputs(rows, h, xs, meta):
    return [xs[0], xs[1], meta] if rows.first_layer else [h]


def _in_proj_kernel(rows, *refs):
    row_refs, refs = refs[:rows.n_refs()], refs[rows.n_refs():]
    gain_ref, w_ref, tab_ref, qg_ref, kg_ref = refs[:5]
    outs = dict(zip(IN_NAMES, refs[5:]))
    offsets = {}
    lo = 0
    for name, width in zip(IN_NAMES, IN_SPLITS):
        offsets[name] = (lo, width)
        lo += width

    def tab(i):
        return tab_ref[:, i * LANES:(i + 1) * LANES]

    def ret_rope(x, scale):
        y = x * tab(TAB_RC) + pltpu.roll(x, RET_QK_DIM // 2, 1) * tab(TAB_RS)
        return y if scale is None else y * scale

    def axial(x, gain, scale):
        y = _rms_rows(x) * gain
        y = (y * tab(TAB_AC) + pltpu.roll(y, LANES - 32, 1) * tab(TAB_ASP)
             + pltpu.roll(y, 32, 1) * tab(TAB_ASM))
        return y if scale is None else y * scale

    x = rows.load(row_refs)
    xg = (x * gain_ref[...]).astype(BF16)
    inv_rms = lax.rsqrt(jnp.mean(x * x, axis=-1, keepdims=True) + NORM_EPS)

    def proj(name):
        lo, width = offsets[name]
        return jnp.dot(xg, w_ref[:, lo:lo + width], preferred_element_type=F32) * inv_rms

    def heads(name, fn):
        y = proj(name)
        for hd in range(y.shape[1] // LANES):
            sl = slice(hd * LANES, (hd + 1) * LANES)
            outs[name][0, :, sl] = fn(y[:, sl]).astype(outs[name].dtype)

    heads("aq", lambda y: axial(y, qg_ref[...], ATT_Q_SCALE))
    heads("ak", lambda y: axial(y, kg_ref[...], None))
    heads("rq", lambda y: ret_rope(y, None))
    heads("rk", lambda y: ret_rope(y, RET_QK_DIM ** -0.5))
    for name in ("av", "rv", "rg", "gr", "ga"):
        outs[name][0] = proj(name).astype(outs[name].dtype)


def _in_proj(rows, h, xs, meta, gain, w_in, tabs, q_gain, k_gain, b):
    tm, nt = rows.tm, rows.nt

    def out_rows(w):
        return pl.BlockSpec((1, tm, w), lambda i, j: (i, j, 0))

    return pl.pallas_call(
        functools.partial(_in_proj_kernel, rows),
        grid=(b, nt),
        in_specs=rows.specs() + [_resident((1, D_MODEL)), _resident((D_MODEL, IN_WIDTH)),
                                 pl.BlockSpec((tm, TAB_W), lambda i, j: (j, 0)),
                                 _resident((1, LANES)), _resident((1, LANES))],
        out_specs=[out_rows(w) for w in IN_SPLITS],
        out_shape=[jax.ShapeDtypeStruct((b, nt * tm, w), BF16) for w in IN_SPLITS],
        compiler_params=_params(2),
        name="in_proj",
    )(*_row_inputs(rows, h, xs, meta), gain, w_in, tabs, q_gain, k_gain)


(COEF_DECAY, COEF_Q_FWD, COEF_Q_BWD, COEF_K_FWD, COEF_K_BWD,
 COEF_Q_FWD_META, COEF_Q_BWD_META, COEF_K_FWD_META, COEF_K_BWD_META) = range(9)
N_COEF = 9
FWD, BWD = 0, 1


def _retention_kernel(lg_ref, q_ref, k_ref, v_ref, g_ref, o_ref, coef_ref, state_ref, stack_ref):
    s_rows = q_ref.shape[1] - N_META
    nc = s_rows // BLOCK + 1
    row = lax.broadcasted_iota(jnp.int32, (BLOCK, BLOCK), 0).astype(F32)
    col = lax.broadcasted_iota(jnp.int32, (BLOCK, BLOCK), 1).astype(F32)
    rel = row - col
    meta_pos = jnp.minimum(row + PAD, BLOCK - 1.0)
    chunk_decay = []
    for hd in range(RET_HEADS):
        log_f, log_b = lg_ref[FWD, hd], lg_ref[BWD, hd]
        coef_ref[hd, COEF_DECAY] = jnp.where(rel >= 0, jnp.exp(log_f * jnp.maximum(rel, 0.0)),
                                             jnp.exp(log_b * jnp.maximum(-rel, 0.0)))
        for pos, shift in ((row, 0), (meta_pos, COEF_Q_FWD_META - COEF_Q_FWD)):
            coef_ref[hd, COEF_Q_FWD + shift] = jnp.exp(log_f * (pos + 1.0))
            coef_ref[hd, COEF_Q_BWD + shift] = jnp.exp(log_b * (BLOCK - pos))
            coef_ref[hd, COEF_K_FWD + shift] = jnp.exp(log_f * (BLOCK - 1.0 - pos))
            coef_ref[hd, COEF_K_BWD + shift] = jnp.exp(log_b * pos)
        ones_row = jnp.ones((1, RET_V_DIM), F32)
        chunk_decay.append((jnp.exp(log_f * BLOCK * ones_row), jnp.exp(log_b * BLOCK * ones_row)))
    state_ref[...] = jnp.zeros_like(state_ref)

    def coef_shift(c):
        return COEF_Q_FWD_META - COEF_Q_FWD if c == 0 else 0

    def load(ref, c, cols):
        if c == 0:
            meta = ref[0, s_rows:s_rows + N_META, cols]
            return jnp.concatenate([meta, jnp.zeros((PAD, meta.shape[1]), meta.dtype)], axis=0)
        return ref[0, (c - 1) * BLOCK:c * BLOCK, cols]

    def qk_cols(hd):
        return slice(hd * RET_QK_DIM, (hd + 1) * RET_QK_DIM)

    def v_cols(hd):
        return slice(hd * RET_V_DIM, (hd + 1) * RET_V_DIM)

    def stack_rows(c, hd, direction=None):
        base = (c * RET_HEADS + hd) * 2 * BLOCK
        if direction is None:
            return slice(base, base + 2 * BLOCK)
        return slice(base + direction * BLOCK, base + (direction + 1) * BLOCK)

    def scan_step(c, direction, update):
        shift = coef_shift(c)
        for hd in range(RET_HEADS):
            slot = direction * RET_HEADS + hd
            state = state_ref[slot]
            stack_ref[stack_rows(c, hd, direction), :] = state.astype(stack_ref.dtype)
            if update:
                k = (load(k_ref, c, qk_cols(hd)).astype(F32)
                     * coef_ref[hd, COEF_K_FWD + direction + shift])
                pushed = lax.dot_general(k.astype(BF16), load(v_ref, c, v_cols(hd)),
                                         (((0,), (0,)), ((), ())), preferred_element_type=F32)
                state_ref[slot] = state * chunk_decay[hd][direction] + pushed

    for t in range(nc):
        scan_step(t, FWD, update=t < nc - 1)
        scan_step(nc - 1 - t, BWD, update=t < nc - 1)

    def mixed(c, hd):
        shift = coef_shift(c)
        q, k, v = load(q_ref, c, qk_cols(hd)), load(k_ref, c, qk_cols(hd)), load(v_ref, c, v_cols(hd))
        s = lax.dot_general(q, k, (((1,), (1,)), ((), ())), preferred_element_type=F32)
        qf = q.astype(F32)
        lhs = jnp.concatenate([(s * coef_ref[hd, COEF_DECAY]).astype(BF16),
                               (qf * coef_ref[hd, COEF_Q_FWD + shift]).astype(BF16),
                               (qf * coef_ref[hd, COEF_Q_BWD + shift]).astype(BF16)], axis=1)
        return lhs, jnp.concatenate([v, stack_ref[stack_rows(c, hd), :]], axis=0)

    items = [(c, hd) for c in range(nc) for hd in range(RET_HEADS)]
    nxt = mixed(*items[0])
    for idx, (c, hd) in enumerate(items):
        lhs, rhs = nxt
        if idx + 1 < len(items):
            nxt = mixed(*items[idx + 1])
        o = _rms_rows(jnp.dot(lhs, rhs, preferred_element_type=F32))
        g = load(g_ref, c, v_cols(hd)).astype(F32)
        o = (o * (g * _sigmoid(g))).astype(o_ref.dtype)
        if c == 0:
            o_ref[0, s_rows:s_rows + N_META, v_cols(hd)] = o[:N_META]
        else:
            o_ref[0, (c - 1) * BLOCK:c * BLOCK, v_cols(hd)] = o


def _retention(log_gamma, rq, rk, rv, rg):
    b, lh, _ = rq.shape
    nc = (lh - N_META) // BLOCK + 1

    def seq(w):
        return pl.BlockSpec((1, lh, w), lambda i, lg: (i, 0, 0))

    return pl.pallas_call(
        _retention_kernel,
        grid_spec=pltpu.PrefetchScalarGridSpec(
            num_scalar_prefetch=1,
            grid=(b,),
            in_specs=[seq(RET_QK_W), seq(RET_QK_W), seq(RET_V_W), seq(RET_V_W)],
            out_specs=seq(RET_V_W),
            scratch_shapes=[pltpu.VMEM((RET_HEADS, N_COEF, BLOCK, BLOCK), F32),
                            pltpu.VMEM((2 * RET_HEADS, RET_QK_DIM, RET_V_DIM), F32),
                            pltpu.VMEM((nc * RET_HEADS * 2 * BLOCK, RET_V_DIM), BF16)],
        ),
        out_shape=jax.ShapeDtypeStruct((b, lh, RET_V_W), BF16),
        compiler_params=_params(1),
        name="retention",
    )(log_gamma, rq, rk, rv, rg)


def _attention_kernel(q_ref, k_ref, v_ref, o_ref):
    tq = q_ref.shape[1]
    lh = k_ref.shape[1]
    lp = lh + PAD
    sub = _row_tile(tq, ATT_SUB_ROWS)
    key = lax.broadcasted_iota(jnp.int32, (1, lp), 1)
    zeros = jnp.zeros((PAD, ATT_HEAD_DIM), k_ref.dtype)
    k = jnp.concatenate([k_ref[0], zeros], axis=0)
    v1 = jnp.concatenate([jnp.concatenate([v_ref[0], zeros], axis=0),
                          jnp.ones((lp, LANES), v_ref.dtype)], axis=1)
    chains = [(slice(r0, r0 + sub), slice(g * LANES, (g + 1) * LANES))
              for r0 in range(0, tq, sub) for g in range(ATT_GROUP)]

    def scores(c):
        rows, cols = chains[c]
        s = lax.dot_general(q_ref[0, rows, cols], k, (((1,), (1,)), ((), ())),
                            preferred_element_type=F32)
        return jnp.where(key < lh, s, MASK_VALUE)

    s_next = scores(0)
    for c, (rows, cols) in enumerate(chains):
        s = s_next
        if c + 1 < len(chains):
            s_next = scores(c + 1)
        p = jnp.exp2(s - jnp.max(s, axis=-1, keepdims=True))
        ov = jnp.dot(p.astype(BF16), v1, preferred_element_type=F32)
        o = ov[:, :ATT_HEAD_DIM] * (1.0 / ov[:, ATT_HEAD_DIM:])
        o_ref[0, rows, cols] = o.astype(o_ref.dtype)


def _attention(aq, ak, av):
    b, lh, _ = aq.shape
    tq = min(ATT_Q_ROWS, lh)
    return pl.pallas_call(
        _attention_kernel,
        grid=(b, ATT_KV_HEADS, pl.cdiv(lh, tq)),
        in_specs=[pl.BlockSpec((1, tq, ATT_GROUP_W), lambda i, j, t: (i, t, j)),
                  pl.BlockSpec((1, lh, ATT_HEAD_DIM), lambda i, j, t: (i, 0, j)),
                  pl.BlockSpec((1, lh, ATT_HEAD_DIM), lambda i, j, t: (i, 0, j))],
        out_specs=pl.BlockSpec((1, tq, ATT_GROUP_W), lambda i, j, t: (i, t, j)),
        out_shape=jax.ShapeDtypeStruct((b, lh, ATT_Q_W), BF16),
        compiler_params=_params(3),
        name="attention",
    )(aq, ak, av)


def _mix_out_kernel(rows, *refs):
    row_refs, refs = refs[:rows.n_refs()], refs[rows.n_refs():]
    ret_ref, att_ref, gr_ref, ga_ref, wr_ref, wa_ref, wo_ref, o_ref = refs
    r = jnp.dot(ret_ref[0], wr_ref[...], preferred_element_type=F32)
    a = jnp.dot(att_ref[0], wa_ref[...], preferred_element_type=F32)
    merged = _sigmoid(gr_ref[0].astype(F32)) * r + _sigmoid(ga_ref[0].astype(F32)) * a
    o_ref[0] = rows.load(row_refs) + jnp.dot(merged.astype(BF16), wo_ref[...],
                                             preferred_element_type=F32)


def _mix_out(rows, h, xs, meta, ret, att, gr, ga, w_ret_o, w_att_o, w_out, b):
    tm, nt = rows.tm, rows.nt
    tile = pl.BlockSpec((1, tm, D_MODEL), lambda i, j: (i, j, 0))
    w = _resident((D_MODEL, D_MODEL))
    return pl.pallas_call(
        functools.partial(_mix_out_kernel, rows),
        grid=(b, nt),
        in_specs=rows.specs() + [tile] * 4 + [w] * 3,
        out_specs=tile,
        out_shape=jax.ShapeDtypeStruct((b, nt * tm, D_MODEL), F32),
        input_output_aliases={} if rows.first_layer else {0: 0},
        compiler_params=_params(2),
        name="mix_out",
    )(*_row_inputs(rows, h, xs, meta), ret, att, gr, ga, w_ret_o, w_att_o, w_out)


def _ffn_kernel(is_final, h_ref, gain_ref, wi_ref, wo_ref, *rest):
    x = h_ref[0]
    xg = (x * gain_ref[...]).astype(BF16)
    inv_rms = lax.rsqrt(jnp.mean(x * x, axis=-1, keepdims=True) + NORM_EPS)
    def pair(j):
        lo = j * FFN_CHUNK
        return (jnp.dot(xg, wi_ref[:, lo:lo + FFN_CHUNK], preferred_element_type=F32),
                jnp.dot(xg, wi_ref[:, D_FF + lo:D_FF + lo + FFN_CHUNK],
                        preferred_element_type=F32))

    acts = []
    nxt = pair(0)
    for j in range(D_FF // FFN_CHUNK):
        a, u = nxt
        if j + 1 < D_FF // FFN_CHUNK:
            nxt = pair(j + 1)
        a = a * inv_rms
        acts.append((a * _sigmoid(a) * (u * inv_rms)).astype(BF16))
    act = jnp.concatenate(acts, axis=1)
    out = x + jnp.dot(act, wo_ref[...], preferred_element_type=F32)
    if not is_final:
        (o_ref,) = rest
        o_ref[0] = out
    else:
        fgain_ref, y_ref = rest
        y_ref[0] = _rms_rows(out) * fgain_ref[...]


def _ffn(h, gain, w_ffn_in, w_ffn_out, tm, final=None):
    b, lh, _ = h.shape
    nt = lh // tm
    tile = pl.BlockSpec((1, tm, D_MODEL), lambda i, j: (i, j, 0))
    weights = [_resident((1, D_MODEL)), _resident((D_MODEL, 2 * D_FF)), _resident((D_FF, D_MODEL))]
    if final is None:
        return pl.pallas_call(
            functools.partial(_ffn_kernel, False),
            grid=(b, nt),
            in_specs=[tile] + weights,
            out_specs=tile,
            out_shape=jax.ShapeDtypeStruct(h.shape, F32),
            input_output_aliases={0: 0},
            compiler_params=_params(2),
            name="ffn",
        )(h, gain, w_ffn_in, w_ffn_out)
    fgain, counts, s = final
    outs, first = [], 0
    for count in counts:
        outs.append(pl.pallas_call(
            functools.partial(_ffn_kernel, True),
            grid=(count, nt),
            in_specs=[pl.BlockSpec((1, tm, D_MODEL), lambda i, j, first=first: (i + first, j, 0))]
            + weights + [_resident((1, D_MODEL))],
            out_specs=tile,
            out_shape=jax.ShapeDtypeStruct((count, s, D_MODEL), F32),
            compiler_params=_params(2),
            name="ffn_final",
        )(h, gain, w_ffn_in, w_ffn_out, fgain))
        first += count
    return outs


def _trunk(xs, meta_tokens, norm_mix, w_in, ret_decay, q_norm, k_norm, w_ret_o, w_att_o, w_out,
           norm_ffn, w_ffn_in, w_ffn_out, norm_final):
    assert len(xs) == 2
    s = xs[0].shape[1]
    assert all(x.shape[1:] == (s, D_MODEL) for x in xs) and s % GRID_W == 0 and s % BLOCK == 0
    lh = s + N_META
    counts = tuple(x.shape[0] for x in xs)
    b = sum(counts)
    tm = _row_tile(lh)
    nt = lh // tm
    assert s > (nt - 1) * tm, "the meta tokens must sit in the last row tile"
    meta = meta_tokens.astype(F32)
    tabs = _rotary_tables(s)
    gamma = 1.0 - jnp.exp2(-ret_decay.astype(F32))
    log_gamma = jnp.log(gamma)
    h = None
    for l in range(DEPTH):
        rows = _Rows(tm, nt, s, counts, first_layer=l == 0)
        rq, rk, rv, rg, aq, ak, av, gr, ga = _in_proj(
            rows, h, xs, meta, norm_mix[l][None], w_in[l].astype(BF16), tabs, q_norm[l][None],
            k_norm[l][None], b)
        ret = _retention(log_gamma[l], rq, rk, rv, rg)
        att = _attention(aq, ak, av)
        h = _mix_out(rows, h, xs, meta, ret, att, gr, ga, w_ret_o[l].astype(BF16),
                     w_att_o[l].astype(BF16), w_out[l].astype(BF16), b)
        final = (norm_final[None], counts, s) if l == DEPTH - 1 else None
        h = _ffn(h, norm_ffn[l][None], w_ffn_in[l].astype(BF16), w_ffn_out[l].astype(BF16), tm,
                 final)
    return h


def kernel(x_prompt, x_sample, meta_tokens, norm_mix, w_in, ret_decay, q_norm, k_norm, w_ret_o,
           w_att_o, w_out, norm_ffn, w_ffn_in, w_ffn_out, norm_final):
    y_prompt, y_sample = _trunk([x_prompt, x_sample], meta_tokens, norm_mix, w_in, ret_decay,
                                q_norm, k_norm, w_ret_o, w_att_o, w_out, norm_ffn, w_ffn_in,
                                w_ffn_out, norm_final)
    return (y_prompt, y_sample)
```

```python
import functools
import math

import jax
import jax.numpy as jnp
from jax import lax
from jax.experimental import pallas as pl
from jax.experimental.pallas import tpu as pltpu

D_MODEL = 1024
DEPTH = 2
N_META = 16
GRID_W = 64
BLOCK = 128
PAD = BLOCK - N_META
RET_HEADS = 4
RET_QK_DIM = 128
RET_V_DIM = 256
ATT_HEADS = 8
ATT_KV_HEADS = 2
ATT_GROUP = ATT_HEADS // ATT_KV_HEADS
ATT_HEAD_DIM = 128
D_FF = ((8 * D_MODEL + 3 * 256 - 1) // (3 * 256)) * 256
ROPE_BASE = 10000.0
NORM_EPS = 1e-6
RET_QK_W = RET_HEADS * RET_QK_DIM
RET_V_W = RET_HEADS * RET_V_DIM
ATT_Q_W = ATT_HEADS * ATT_HEAD_DIM
ATT_KV_W = ATT_KV_HEADS * ATT_HEAD_DIM
ATT_GROUP_W = ATT_GROUP * ATT_HEAD_DIM
IN_NAMES = ("rq", "rk", "rv", "rg", "aq", "ak", "av", "gr", "ga")
IN_SPLITS = (RET_QK_W, RET_QK_W, RET_V_W, RET_V_W, ATT_Q_W, ATT_KV_W, ATT_KV_W, D_MODEL, D_MODEL)
IN_WIDTH = sum(IN_SPLITS)

LANES = 128
BF16_SUBLANES = 16
VMEM_BYTES_V7X = 64 * 1024 * 1024
VMEM_LIMIT = VMEM_BYTES_V7X * 7 // 8
ROW_TILE_TARGET = 688
ATT_Q_ROWS = 1056
ATT_SUB_ROWS = 352
MXU_WIDTH_V7X = 256
FFN_CHUNK = MXU_WIDTH_V7X
MASK_VALUE = -1e30
ATT_Q_SCALE = ATT_HEAD_DIM ** -0.5 * math.log2(math.e)

F32 = jnp.float32
BF16 = jnp.bfloat16

TAB_RC, TAB_RS, TAB_AC, TAB_ASP, TAB_ASM = range(5)
TAB_W = 5 * LANES


def _row_tile(rows, target=None):
    target = ROW_TILE_TARGET if target is None else target
    best = None
    for d in range(BF16_SUBLANES, min(rows, target) + 1, BF16_SUBLANES):
        if rows % d == 0:
            best = d
    assert best is not None, rows
    return best


def _params(n_axes):
    return pltpu.CompilerParams(dimension_semantics=("arbitrary",) * n_axes,
                                vmem_limit_bytes=VMEM_LIMIT)


def _resident(shape):
    return pl.BlockSpec(shape, lambda *_: (0,) * len(shape), pipeline_mode=pl.Buffered(1))


def _rms_rows(x):
    return x * lax.rsqrt(jnp.mean(x * x, axis=-1, keepdims=True) + NORM_EPS)


def _sigmoid(x):
    return 1.0 / (1.0 + jnp.exp(-x))


def _rotary_tables(s):
    r = jnp.arange(s + N_META, dtype=jnp.int32)
    is_tok = r < s
    t = jnp.where(is_tok, r + N_META, r - s)
    ret_inv = ROPE_BASE ** (-jnp.linspace(0.0, 1.0, RET_QK_DIM // 2, dtype=F32))
    ret_ang = t.astype(F32)[:, None] * ret_inv[None, :]
    rc, rs = jnp.cos(ret_ang), jnp.sin(ret_ang)
    row = jnp.where(is_tok, r // GRID_W, 0).astype(F32)
    col = jnp.where(is_tok, r % GRID_W, 0).astype(F32)
    ax_half = ATT_HEAD_DIM // 2
    ax_inv = ROPE_BASE ** (-jnp.arange(ax_half // 2, dtype=F32) * 2.0 / ax_half)
    ra, ca = row[:, None] * ax_inv[None, :], col[:, None] * ax_inv[None, :]
    z = jnp.zeros_like(ra)
    return jnp.concatenate([
        rc, rc,
        -rs, rs,
        jnp.cos(ra), jnp.cos(ra), jnp.cos(ca), jnp.cos(ca),
        -jnp.sin(ra), z, -jnp.sin(ca), z,
        z, jnp.sin(ra), z, jnp.sin(ca),
    ], axis=1)


class _Rows:
    def __init__(self, tm, nt, s, counts, first_layer):
        self.tm, self.nt, self.s, self.counts, self.first_layer = tm, nt, s, counts, first_layer

    def specs(self):
        tm, nt = self.tm, self.nt
        if not self.first_layer:
            return [pl.BlockSpec((1, tm, D_MODEL), lambda i, j: (i, j, 0))]
        n0 = self.counts[0]

        def first(i, j):
            mine = i < n0
            return (jnp.where(mine, i, n0 - 1), jnp.where(mine, j, nt - 1), 0)

        def second(i, j):
            mine = i >= n0
            return (jnp.where(mine, i - n0, 0), jnp.where(mine, j, 0), 0)

        return [pl.BlockSpec((1, tm, D_MODEL), first), pl.BlockSpec((1, tm, D_MODEL), second),
                _resident((N_META, D_MODEL))]

    def n_refs(self):
        return 3 if self.first_layer else 1

    def load(self, refs):
        if not self.first_layer:
            return refs[0][0]
        xa_ref, xb_ref, meta_ref = refs
        i, j = pl.program_id(0), pl.program_id(1)
        x = jnp.where(i < self.counts[0], xa_ref[0], xb_ref[0])
        meta_at = self.s - (self.nt - 1) * self.tm
        assert meta_at + N_META == self.tm
        meta_tile = jnp.concatenate([jnp.zeros((meta_at, D_MODEL), F32), meta_ref[...]], axis=0)
        row = j * self.tm + lax.broadcasted_iota(jnp.int32, (self.tm, 1), 0)
        return jnp.where(row < self.s, x, meta_tile)


def _row_inputs(rows, h, xs, meta):
    return [xs[0], xs[1], meta] if rows.first_layer else [h]


def _in_proj_kernel(rows, *refs):
    row_refs, refs = refs[:rows.n_refs()], refs[rows.n_refs():]
    gain_ref, w_ref, tab_ref, qg_ref, kg_ref = refs[:5]
    outs = dict(zip(IN_NAMES, refs[5:]))
    offsets = {}
    lo = 0
    for name, width in zip(IN_NAMES, IN_SPLITS):
        offsets[name] = (lo, width)
        lo += width

    def tab(i):
        return tab_ref[:, i * LANES:(i + 1) * LANES]

    def ret_rope(x, scale):
        y = x * tab(TAB_RC) + pltpu.roll(x, RET_QK_DIM // 2, 1) * tab(TAB_RS)
        return y if scale is None else y * scale

    def axial(x, gain, scale):
        y = _rms_rows(x) * gain
        y = (y * tab(TAB_AC) + pltpu.roll(y, LANES - 32, 1) * tab(TAB_ASP)
             + pltpu.roll(y, 32, 1) * tab(TAB_ASM))
        return y if scale is None else y * scale

    x = rows.load(row_refs)
    xg = (x * gain_ref[...]).astype(BF16)
    inv_rms = lax.rsqrt(jnp.mean(x * x, axis=-1, keepdims=True) + NORM_EPS)

    def proj(name):
        lo, width = offsets[name]
        return jnp.dot(xg, w_ref[:, lo:lo + width], preferred_element_type=F32) * inv_rms

    def heads(name, fn):
        y = proj(name)
        for hd in range(y.shape[1] // LANES):
            sl = slice(hd * LANES, (hd + 1) * LANES)
            outs[name][0, :, sl] = fn(y[:, sl]).astype(outs[name].dtype)

    def plain(name):
        outs[name][0] = proj(name).astype(outs[name].dtype)

    heads("aq", lambda y: axial(y, qg_ref[...], ATT_Q_SCALE))
    plain("rv")
    heads("ak", lambda y: axial(y, kg_ref[...], None))
    plain("rg")
    heads("rq", lambda y: ret_rope(y, None))
    plain("gr")
    heads("rk", lambda y: ret_rope(y, RET_QK_DIM ** -0.5))
    plain("ga")
    plain("av")


def _in_proj(rows, h, xs, meta, gain, w_in, tabs, q_gain, k_gain, b):
    tm, nt = rows.tm, rows.nt

    def out_rows(w):
        return pl.BlockSpec((1, tm, w), lambda i, j: (i, j, 0))

    return pl.pallas_call(
        functools.partial(_in_proj_kernel, rows),
        grid=(b, nt),
        in_specs=rows.specs() + [_resident((1, D_MODEL)), _resident((D_MODEL, IN_WIDTH)),
                                 pl.BlockSpec((tm, TAB_W), lambda i, j: (j, 0)),
                                 _resident((1, LANES)), _resident((1, LANES))],
        out_specs=[out_rows(w) for w in IN_SPLITS],
        out_shape=[jax.ShapeDtypeStruct((b, nt * tm, w), BF16) for w in IN_SPLITS],
        compiler_params=_params(2),
        name="in_proj",
    )(*_row_inputs(rows, h, xs, meta), gain, w_in, tabs, q_gain, k_gain)


(COEF_DECAY, COEF_Q_FWD, COEF_Q_BWD, COEF_K_FWD, COEF_K_BWD,
 COEF_Q_FWD_META, COEF_Q_BWD_META, COEF_K_FWD_META, COEF_K_BWD_META) = range(9)
N_COEF = 9
FWD, BWD = 0, 1


def _retention_kernel(lg_ref, q_ref, k_ref, v_ref, g_ref, o_ref, coef_ref, state_ref, stack_ref):
    s_rows = q_ref.shape[1] - N_META
    nc = s_rows // BLOCK + 1
    row = lax.broadcasted_iota(jnp.int32, (BLOCK, BLOCK), 0).astype(F32)
    col = lax.broadcasted_iota(jnp.int32, (BLOCK, BLOCK), 1).astype(F32)
    rel = row - col
    meta_pos = jnp.minimum(row + PAD, BLOCK - 1.0)
    chunk_decay = []
    for hd in range(RET_HEADS):
        log_f, log_b = lg_ref[FWD, hd], lg_ref[BWD, hd]
        coef_ref[hd, COEF_DECAY] = jnp.where(rel >= 0, jnp.exp(log_f * jnp.maximum(rel, 0.0)),
                                             jnp.exp(log_b * jnp.maximum(-rel, 0.0)))
        for pos, shift in ((row, 0), (meta_pos, COEF_Q_FWD_META - COEF_Q_FWD)):
            coef_ref[hd, COEF_Q_FWD + shift] = jnp.exp(log_f * (pos + 1.0))
            coef_ref[hd, COEF_Q_BWD + shift] = jnp.exp(log_b * (BLOCK - pos))
            coef_ref[hd, COEF_K_FWD + shift] = jnp.exp(log_f * (BLOCK - 1.0 - pos))
            coef_ref[hd, COEF_K_BWD + shift] = jnp.exp(log_b * pos)
        ones_row = jnp.ones((1, RET_V_DIM), F32)
        chunk_decay.append((jnp.exp(log_f * BLOCK * ones_row), jnp.exp(log_b * BLOCK * ones_row)))
    state_ref[...] = jnp.zeros_like(state_ref)

    def coef_shift(c):
        return COEF_Q_FWD_META - COEF_Q_FWD if c == 0 else 0

    def load(ref, c, cols):
        if c == 0:
            meta = ref[0, s_rows:s_rows + N_META, cols]
            return jnp.concatenate([meta, jnp.zeros((PAD, meta.shape[1]), meta.dtype)], axis=0)
        return ref[0, (c - 1) * BLOCK:c * BLOCK, cols]

    def qk_cols(hd):
        return slice(hd * RET_QK_DIM, (hd + 1) * RET_QK_DIM)

    def v_cols(hd):
        return slice(hd * RET_V_DIM, (hd + 1) * RET_V_DIM)

    def stack_rows(c, hd, direction=None):
        base = (c * RET_HEADS + hd) * 2 * BLOCK
        if direction is None:
            return slice(base, base + 2 * BLOCK)
        return slice(base + direction * BLOCK, base + (direction + 1) * BLOCK)

    def scan_step(c, direction, update):
        shift = coef_shift(c)
        for hd in range(RET_HEADS):
            slot = direction * RET_HEADS + hd
            state = state_ref[slot]
            stack_ref[stack_rows(c, hd, direction), :] = state.astype(stack_ref.dtype)
            if update:
                k = (load(k_ref, c, qk_cols(hd)).astype(F32)
                     * coef_ref[hd, COEF_K_FWD + direction + shift])
                pushed = lax.dot_general(k.astype(BF16), load(v_ref, c, v_cols(hd)),
                                         (((0,), (0,)), ((), ())), preferred_element_type=F32)
                state_ref[slot] = state * chunk_decay[hd][direction] + pushed

    for t in range(nc):
        scan_step(t, FWD, update=t < nc - 1)
        scan_step(nc - 1 - t, BWD, update=t < nc - 1)

    def mixed(c, hd):
        shift = coef_shift(c)
        q, k, v = load(q_ref, c, qk_cols(hd)), load(k_ref, c, qk_cols(hd)), load(v_ref, c, v_cols(hd))
        s = lax.dot_general(q, k, (((1,), (1,)), ((), ())), preferred_element_type=F32)
        qf = q.astype(F32)
        lhs = jnp.concatenate([(s * coef_ref[hd, COEF_DECAY]).astype(BF16),
                               (qf * coef_ref[hd, COEF_Q_FWD + shift]).astype(BF16),
                               (qf * coef_ref[hd, COEF_Q_BWD + shift]).astype(BF16)], axis=1)
        return lhs, jnp.concatenate([v, stack_ref[stack_rows(c, hd), :]], axis=0)

    items = [(c, hd) for c in range(nc) for hd in range(RET_HEADS)]
    nxt = mixed(*items[0])
    for idx, (c, hd) in enumerate(items):
        lhs, rhs = nxt
        if idx + 1 < len(items):
            nxt = mixed(*items[idx + 1])
        o = _rms_rows(jnp.dot(lhs, rhs, preferred_element_type=F32))
        g = load(g_ref, c, v_cols(hd)).astype(F32)
        o = (o * (g * _sigmoid(g))).astype(o_ref.dtype)
        if c == 0:
            o_ref[0, s_rows:s_rows + N_META, v_cols(hd)] = o[:N_META]
        else:
            o_ref[0, (c - 1) * BLOCK:c * BLOCK, v_cols(hd)] = o


def _retention(log_gamma, rq, rk, rv, rg):
    b, lh, _ = rq.shape
    nc = (lh - N_META) // BLOCK + 1

    def seq(w):
        return pl.BlockSpec((1, lh, w), lambda i, lg: (i, 0, 0))

    return pl.pallas_call(
        _retention_kernel,
        grid_spec=pltpu.PrefetchScalarGridSpec(
            num_scalar_prefetch=1,
            grid=(b,),
            in_specs=[seq(RET_QK_W), seq(RET_QK_W), seq(RET_V_W), seq(RET_V_W)],
            out_specs=seq(RET_V_W),
            scratch_shapes=[pltpu.VMEM((RET_HEADS, N_COEF, BLOCK, BLOCK), F32),
                            pltpu.VMEM((2 * RET_HEADS, RET_QK_DIM, RET_V_DIM), F32),
                            pltpu.VMEM((nc * RET_HEADS * 2 * BLOCK, RET_V_DIM), BF16)],
        ),
        out_shape=jax.ShapeDtypeStruct((b, lh, RET_V_W), BF16),
        compiler_params=_params(1),
        name="retention",
    )(log_gamma, rq, rk, rv, rg)


def _attention_kernel(q_ref, k_ref, v_ref, o_ref):
    tq = q_ref.shape[1]
    lh = k_ref.shape[1]
    lp = lh + PAD
    sub = _row_tile(tq, ATT_SUB_ROWS)
    key = lax.broadcasted_iota(jnp.int32, (1, lp), 1)
    zeros = jnp.zeros((PAD, ATT_HEAD_DIM), k_ref.dtype)
    k = jnp.concatenate([k_ref[0], zeros], axis=0)
    v1 = jnp.concatenate([jnp.concatenate([v_ref[0], zeros], axis=0),
                          jnp.ones((lp, LANES), v_ref.dtype)], axis=1)
    chains = [(slice(r0, r0 + sub), slice(g * LANES, (g + 1) * LANES))
              for r0 in range(0, tq, sub) for g in range(ATT_GROUP)]

    def scores(c):
        rows, cols = chains[c]
        s = lax.dot_general(q_ref[0, rows, cols], k, (((1,), (1,)), ((), ())),
                            preferred_element_type=F32)
        return jnp.where(key < lh, s, MASK_VALUE)

    s_next = scores(0)
    for c, (rows, cols) in enumerate(chains):
        s = s_next
        if c + 1 < len(chains):
            s_next = scores(c + 1)
        p = jnp.exp2(s - jnp.max(s, axis=-1, keepdims=True))
        ov = jnp.dot(p.astype(BF16), v1, preferred_element_type=F32)
        o = ov[:, :ATT_HEAD_DIM] * (1.0 / ov[:, ATT_HEAD_DIM:])
        o_ref[0, rows, cols] = o.astype(o_ref.dtype)


def _attention(aq, ak, av):
    b, lh, _ = aq.shape
    tq = min(ATT_Q_ROWS, lh)
    return pl.pallas_call(
        _attention_kernel,
        grid=(b, ATT_KV_HEADS, pl.cdiv(lh, tq)),
        in_specs=[pl.BlockSpec((1, tq, ATT_GROUP_W), lambda i, j, t: (i, t, j)),
                  pl.BlockSpec((1, lh, ATT_HEAD_DIM), lambda i, j, t: (i, 0, j)),
                  pl.BlockSpec((1, lh, ATT_HEAD_DIM), lambda i, j, t: (i, 0, j))],
        out_specs=pl.BlockSpec((1, tq, ATT_GROUP_W), lambda i, j, t: (i, t, j)),
        out_shape=jax.ShapeDtypeStruct((b, lh, ATT_Q_W), BF16),
        compiler_params=_params(3),
        name="attention",
    )(aq, ak, av)


def _mix_out_kernel(rows, *refs):
    row_refs, refs = refs[:rows.n_refs()], refs[rows.n_refs():]
    ret_ref, att_ref, gr_ref, ga_ref, wr_ref, wa_ref, wo_ref, o_ref = refs
    r = jnp.dot(ret_ref[0], wr_ref[...], preferred_element_type=F32)
    a = jnp.dot(att_ref[0], wa_ref[...], preferred_element_type=F32)
    merged = _sigmoid(gr_ref[0].astype(F32)) * r + _sigmoid(ga_ref[0].astype(F32)) * a
    o_ref[0] = rows.load(row_refs) + jnp.dot(merged.astype(BF16), wo_ref[...],
                                             preferred_element_type=F32)


def _mix_out(rows, h, xs, meta, ret, att, gr, ga, w_ret_o, w_att_o, w_out, b):
    tm, nt = rows.tm, rows.nt
    tile = pl.BlockSpec((1, tm, D_MODEL), lambda i, j: (i, j, 0))
    w = _resident((D_MODEL, D_MODEL))
    return pl.pallas_call(
        functools.partial(_mix_out_kernel, rows),
        grid=(b, nt),
        in_specs=rows.specs() + [tile] * 4 + [w] * 3,
        out_specs=tile,
        out_shape=jax.ShapeDtypeStruct((b, nt * tm, D_MODEL), F32),
        input_output_aliases={} if rows.first_layer else {0: 0},
        compiler_params=_params(2),
        name="mix_out",
    )(*_row_inputs(rows, h, xs, meta), ret, att, gr, ga, w_ret_o, w_att_o, w_out)


def _ffn_kernel(is_final, h_ref, gain_ref, wi_ref, wo_ref, *rest):
    x = h_ref[0]
    xg = (x * gain_ref[...]).astype(BF16)
    inv_rms = lax.rsqrt(jnp.mean(x * x, axis=-1, keepdims=True) + NORM_EPS)
    def pair(j):
        lo = j * FFN_CHUNK
        return (jnp.dot(xg, wi_ref[:, lo:lo + FFN_CHUNK], preferred_element_type=F32),
                jnp.dot(xg, wi_ref[:, D_FF + lo:D_FF + lo + FFN_CHUNK],
                        preferred_element_type=F32))

    acts = []
    nxt = pair(0)
    for j in range(D_FF // FFN_CHUNK):
        a, u = nxt
        if j + 1 < D_FF // FFN_CHUNK:
            nxt = pair(j + 1)
        a = a * inv_rms
        acts.append((a * _sigmoid(a) * (u * inv_rms)).astype(BF16))
    act = jnp.concatenate(acts, axis=1)
    out = x + jnp.dot(act, wo_ref[...], preferred_element_type=F32)
    if not is_final:
        (o_ref,) = rest
        o_ref[0] = out
    else:
        fgain_ref, y_ref = rest
        y_ref[0] = _rms_rows(out) * fgain_ref[...]


def _ffn(h, gain, w_ffn_in, w_ffn_out, tm, final=None):
    b, lh, _ = h.shape
    nt = lh // tm
    tile = pl.BlockSpec((1, tm, D_MODEL), lambda i, j: (i, j, 0))
    weights = [_resident((1, D_MODEL)), _resident((D_MODEL, 2 * D_FF)), _resident((D_FF, D_MODEL))]
    if final is None:
        return pl.pallas_call(
            functools.partial(_ffn_kernel, False),
            grid=(b, nt),
            in_specs=[tile] + weights,
            out_specs=tile,
            out_shape=jax.ShapeDtypeStruct(h.shape, F32),
            input_output_aliases={0: 0},
            compiler_params=_params(2),
            name="ffn",
        )(h, gain, w_ffn_in, w_ffn_out)
    fgain, counts, s = final
    outs, first = [], 0
    for count in counts:
        outs.append(pl.pallas_call(
            functools.partial(_ffn_kernel, True),
            grid=(count, nt),
            in_specs=[pl.BlockSpec((1, tm, D_MODEL), lambda i, j, first=first: (i + first, j, 0))]
            + weights + [_resident((1, D_MODEL))],
            out_specs=tile,
            out_shape=jax.ShapeDtypeStruct((count, s, D_MODEL), F32),
            compiler_params=_params(2),
            name="ffn_final",
        )(h, gain, w_ffn_in, w_ffn_out, fgain))
        first += count
    return outs


def _trunk(xs, meta_tokens, norm_mix, w_in, ret_decay, q_norm, k_norm, w_ret_o, w_att_o, w_out,
           norm_ffn, w_ffn_in, w_ffn_out, norm_final):
    assert len(xs) == 2
    s = xs[0].shape[1]
    assert all(x.shape[1:] == (s, D_MODEL) for x in xs) and s % GRID_W == 0 and s % BLOCK == 0
    lh = s + N_META
    counts = tuple(x.shape[0] for x in xs)
    b = sum(counts)
    tm = _row_tile(lh)
    nt = lh // tm
    assert s > (nt - 1) * tm, "the meta tokens must sit in the last row tile"
    meta = meta_tokens.astype(F32)
    tabs = _rotary_tables(s)
    gamma = 1.0 - jnp.exp2(-ret_decay.astype(F32))
    log_gamma = jnp.log(gamma)
    h = None
    for l in range(DEPTH):
        rows = _Rows(tm, nt, s, counts, first_layer=l == 0)
        rq, rk, rv, rg, aq, ak, av, gr, ga = _in_proj(
            rows, h, xs, meta, norm_mix[l][None], w_in[l].astype(BF16), tabs, q_norm[l][None],
            k_norm[l][None], b)
        ret = _retention(log_gamma[l], rq, rk, rv, rg)
        att = _attention(aq, ak, av)
        h = _mix_out(rows, h, xs, meta, ret, att, gr, ga, w_ret_o[l].astype(BF16),
                     w_att_o[l].astype(BF16), w_out[l].astype(BF16), b)
        final = (norm_final[None], counts, s) if l == DEPTH - 1 else None
        h = _ffn(h, norm_ffn[l][None], w_ffn_in[l].astype(BF16), w_ffn_out[l].astype(BF16), tm,
                 final)
    return h


def kernel(x_prompt, x_sample, meta_tokens, norm_mix, w_in, ret_decay, q_norm, k_norm, w_ret_o,
           w_att_o, w_out, norm_ffn, w_ffn_in, w_ffn_out, norm_final):
    y_prompt, y_sample = _trunk([x_prompt, x_sample], meta_tokens, norm_mix, w_in, ret_decay,
                                q_norm, k_norm, w_ret_o, w_att_o, w_out, norm_ffn, w_ffn_in,
                                w_ffn_out, norm_final)
    return (y_prompt, y_sample)
```

```python
import functools
import math

import jax
import jax.numpy as jnp
from jax import lax
from jax.experimental import pallas as pl
from jax.experimental.pallas import tpu as pltpu

D_MODEL = 1024
DEPTH = 2
N_META = 16
GRID_W = 64
BLOCK = 128
PAD = BLOCK - N_META
RET_HEADS = 4
RET_QK_DIM = 128
RET_V_DIM = 256
ATT_HEADS = 8
ATT_KV_HEADS = 2
ATT_GROUP = ATT_HEADS // ATT_KV_HEADS
ATT_HEAD_DIM = 128
D_FF = ((8 * D_MODEL + 3 * 256 - 1) // (3 * 256)) * 256
ROPE_BASE = 10000.0
NORM_EPS = 1e-6
RET_QK_W = RET_HEADS * RET_QK_DIM
RET_V_W = RET_HEADS * RET_V_DIM
ATT_Q_W = ATT_HEADS * ATT_HEAD_DIM
ATT_KV_W = ATT_KV_HEADS * ATT_HEAD_DIM
ATT_GROUP_W = ATT_GROUP * ATT_HEAD_DIM
IN_NAMES = ("rq", "rk", "rv", "rg", "aq", "ak", "av", "gr", "ga")
IN_SPLITS = (RET_QK_W, RET_QK_W, RET_V_W, RET_V_W, ATT_Q_W, ATT_KV_W, ATT_KV_W, D_MODEL, D_MODEL)
IN_WIDTH = sum(IN_SPLITS)

LANES = 128
BF16_SUBLANES = 16
VMEM_BYTES_V7X = 64 * 1024 * 1024
VMEM_LIMIT = VMEM_BYTES_V7X * 7 // 8
ROW_TILE_TARGET = 688
ATT_Q_ROWS = 1056
ATT_SUB_ROWS = 352
MXU_WIDTH_V7X = 256
FFN_CHUNK = MXU_WIDTH_V7X
MASK_VALUE = -1e30
ATT_Q_SCALE = ATT_HEAD_DIM ** -0.5 * math.log2(math.e)

F32 = jnp.float32
BF16 = jnp.bfloat16

TAB_RC, TAB_RS, TAB_AC, TAB_ASP, TAB_ASM = range(5)
TAB_W = 5 * LANES


def _row_tile(rows, target=None):
    target = ROW_TILE_TARGET if target is None else target
    best = None
    for d in range(BF16_SUBLANES, min(rows, target) + 1, BF16_SUBLANES):
        if rows % d == 0:
            best = d
    assert best is not None, rows
    return best


def _params(n_axes):
    return pltpu.CompilerParams(dimension_semantics=("arbitrary",) * n_axes,
                                vmem_limit_bytes=VMEM_LIMIT)


def _resident(shape):
    return pl.BlockSpec(shape, lambda *_: (0,) * len(shape), pipeline_mode=pl.Buffered(1))


def _rms_rows(x):
    return x * lax.rsqrt(jnp.mean(x * x, axis=-1, keepdims=True) + NORM_EPS)


def _sigmoid(x):
    return 1.0 / (1.0 + jnp.exp(-x))


def _rotary_tables(s):
    r = jnp.arange(s + N_META, dtype=jnp.int32)
    is_tok = r < s
    t = jnp.where(is_tok, r + N_META, r - s)
    ret_inv = ROPE_BASE ** (-jnp.linspace(0.0, 1.0, RET_QK_DIM // 2, dtype=F32))
    ret_ang = t.astype(F32)[:, None] * ret_inv[None, :]
    rc, rs = jnp.cos(ret_ang), jnp.sin(ret_ang)
    row = jnp.where(is_tok, r // GRID_W, 0).astype(F32)
    col = jnp.where(is_tok, r % GRID_W, 0).astype(F32)
    ax_half = ATT_HEAD_DIM // 2
    ax_inv = ROPE_BASE ** (-jnp.arange(ax_half // 2, dtype=F32) * 2.0 / ax_half)
    ra, ca = row[:, None] * ax_inv[None, :], col[:, None] * ax_inv[None, :]
    z = jnp.zeros_like(ra)
    return jnp.concatenate([
        rc, rc,
        -rs, rs,
        jnp.cos(ra), jnp.cos(ra), jnp.cos(ca), jnp.cos(ca),
        -jnp.sin(ra), z, -jnp.sin(ca), z,
        z, jnp.sin(ra), z, jnp.sin(ca),
    ], axis=1)


class _Rows:
    def __init__(self, tm, nt, s, counts, first_layer):
        self.tm, self.nt, self.s, self.counts, self.first_layer = tm, nt, s, counts, first_layer

    def specs(self):
        tm, nt = self.tm, self.nt
        if not self.first_layer:
            return [pl.BlockSpec((1, tm, D_MODEL), lambda i, j: (i, j, 0))]
        n0 = self.counts[0]

        def first(i, j):
            mine = i < n0
            return (jnp.where(mine, i, n0 - 1), jnp.where(mine, j, nt - 1), 0)

        def second(i, j):
            mine = i >= n0
            return (jnp.where(mine, i - n0, 0), jnp.where(mine, j, 0), 0)

        return [pl.BlockSpec((1, tm, D_MODEL), first), pl.BlockSpec((1, tm, D_MODEL), second),
                _resident((N_META, D_MODEL))]

    def n_refs(self):
        return 3 if self.first_layer else 1

    def load(self, refs):
        if not self.first_layer:
            return refs[0][0]
        xa_ref, xb_ref, meta_ref = refs
        i, j = pl.program_id(0), pl.program_id(1)
        x = jnp.where(i < self.counts[0], xa_ref[0], xb_ref[0])
        meta_at = self.s - (self.nt - 1) * self.tm
        assert meta_at + N_META == self.tm
        meta_tile = jnp.concatenate([jnp.zeros((meta_at, D_MODEL), F32), meta_ref[...]], axis=0)
        row = j * self.tm + lax.broadcasted_iota(jnp.int32, (self.tm, 1), 0)
        return jnp.where(row < self.s, x, meta_tile)


def _row_inputs(rows, h, xs, meta):
    return [xs[0], xs[1], meta] if rows.first_layer else [h]


def _in_proj_kernel(rows, *refs):
    row_refs, refs = refs[:rows.n_refs()], refs[rows.n_refs():]
    gain_ref, w_ref, tab_ref, qg_ref, kg_ref = refs[:5]
    outs = dict(zip(IN_NAMES, refs[5:]))
    offsets = {}
    lo = 0
    for name, width in zip(IN_NAMES, IN_SPLITS):
        offsets[name] = (lo, width)
        lo += width

    def tab(i):
        return tab_ref[:, i * LANES:(i + 1) * LANES]

    def ret_rope(x, scale):
        y = x * tab(TAB_RC) + pltpu.roll(x, RET_QK_DIM // 2, 1) * tab(TAB_RS)
        return y if scale is None else y * scale

    def axial(x, gain, scale):
        y = _rms_rows(x) * gain
        y = (y * tab(TAB_AC) + pltpu.roll(y, LANES - 32, 1) * tab(TAB_ASP)
             + pltpu.roll(y, 32, 1) * tab(TAB_ASM))
        return y if scale is None else y * scale

    x = rows.load(row_refs)
    xg = (x * gain_ref[...]).astype(BF16)
    inv_rms = lax.rsqrt(jnp.mean(x * x, axis=-1, keepdims=True) + NORM_EPS)

    def proj(name):
        lo, width = offsets[name]
        return jnp.dot(xg, w_ref[:, lo:lo + width], preferred_element_type=F32) * inv_rms

    def heads(name, fn):
        y = proj(name)
        for hd in range(y.shape[1] // LANES):
            sl = slice(hd * LANES, (hd + 1) * LANES)
            outs[name][0, :, sl] = fn(y[:, sl]).astype(outs[name].dtype)

    def plain(name):
        outs[name][0] = proj(name).astype(outs[name].dtype)

    heads("aq", lambda y: axial(y, qg_ref[...], ATT_Q_SCALE))
    plain("rv")
    heads("ak", lambda y: axial(y, kg_ref[...], None))
    plain("rg")
    heads("rq", lambda y: ret_rope(y, None))
    plain("gr")
    heads("rk", lambda y: ret_rope(y, RET_QK_DIM ** -0.5))
    plain("ga")
    plain("av")


def _in_proj(rows, h, xs, meta, gain, w_in, tabs, q_gain, k_gain, b):
    tm, nt = rows.tm, rows.nt

    def out_rows(w):
        return pl.BlockSpec((1, tm, w), lambda i, j: (i, j, 0))

    return pl.pallas_call(
        functools.partial(_in_proj_kernel, rows),
        grid=(b, nt),
        in_specs=rows.specs() + [_resident((1, D_MODEL)), _resident((D_MODEL, IN_WIDTH)),
                                 pl.BlockSpec((tm, TAB_W), lambda i, j: (j, 0)),
                                 _resident((1, LANES)), _resident((1, LANES))],
        out_specs=[out_rows(w) for w in IN_SPLITS],
        out_shape=[jax.ShapeDtypeStruct((b, nt * tm, w), BF16) for w in IN_SPLITS],
        compiler_params=_params(2),
        name="in_proj",
    )(*_row_inputs(rows, h, xs, meta), gain, w_in, tabs, q_gain, k_gain)


(COEF_DECAY, COEF_Q_FWD, COEF_Q_BWD, COEF_K_FWD, COEF_K_BWD,
 COEF_Q_FWD_META, COEF_Q_BWD_META, COEF_K_FWD_META, COEF_K_BWD_META) = range(9)
N_COEF = 9
FWD, BWD = 0, 1


def _retention_kernel(lg_ref, q_ref, k_ref, v_ref, g_ref, o_ref, coef_ref, state_ref, stack_ref):
    s_rows = q_ref.shape[1] - N_META
    nc = s_rows // BLOCK + 1
    row = lax.broadcasted_iota(jnp.int32, (BLOCK, BLOCK), 0).astype(F32)
    col = lax.broadcasted_iota(jnp.int32, (BLOCK, BLOCK), 1).astype(F32)
    rel = row - col
    meta_pos = jnp.minimum(row + PAD, BLOCK - 1.0)
    chunk_decay = []
    for hd in range(RET_HEADS):
        log_f, log_b = lg_ref[FWD, hd], lg_ref[BWD, hd]
        coef_ref[hd, COEF_DECAY] = jnp.where(rel >= 0, jnp.exp(log_f * jnp.maximum(rel, 0.0)),
                                             jnp.exp(log_b * jnp.maximum(-rel, 0.0)))
        for pos, shift in ((row, 0), (meta_pos, COEF_Q_FWD_META - COEF_Q_FWD)):
            coef_ref[hd, COEF_Q_FWD + shift] = jnp.exp(log_f * (pos + 1.0))
            coef_ref[hd, COEF_Q_BWD + shift] = jnp.exp(log_b * (BLOCK - pos))
            coef_ref[hd, COEF_K_FWD + shift] = jnp.exp(log_f * (BLOCK - 1.0 - pos))
            coef_ref[hd, COEF_K_BWD + shift] = jnp.exp(log_b * pos)
        ones_row = jnp.ones((1, RET_V_DIM), F32)
        chunk_decay.append((jnp.exp(log_f * BLOCK * ones_row), jnp.exp(log_b * BLOCK * ones_row)))
    state_ref[...] = jnp.zeros_like(state_ref)

    def coef_shift(c):
        return COEF_Q_FWD_META - COEF_Q_FWD if c == 0 else 0

    def load(ref, c, cols):
        if c == 0:
            meta = ref[0, s_rows:s_rows + N_META, cols]
            return jnp.concatenate([meta, jnp.zeros((PAD, meta.shape[1]), meta.dtype)], axis=0)
        return ref[0, (c - 1) * BLOCK:c * BLOCK, cols]

    def qk_cols(hd):
        return slice(hd * RET_QK_DIM, (hd + 1) * RET_QK_DIM)

    def v_cols(hd):
        return slice(hd * RET_V_DIM, (hd + 1) * RET_V_DIM)

    def stack_rows(c, hd, direction=None):
        base = (c * RET_HEADS + hd) * 2 * BLOCK
        if direction is None:
            return slice(base, base + 2 * BLOCK)
        return slice(base + direction * BLOCK, base + (direction + 1) * BLOCK)

    def scan_step(c, direction, hd, update):
        shift = coef_shift(c)
        slot = direction * RET_HEADS + hd
        state = state_ref[slot]
        stack_ref[stack_rows(c, hd, direction), :] = state.astype(stack_ref.dtype)
        if update:
            k = (load(k_ref, c, qk_cols(hd)).astype(F32)
                 * coef_ref[hd, COEF_K_FWD + direction + shift])
            pushed = lax.dot_general(k.astype(BF16), load(v_ref, c, v_cols(hd)),
                                     (((0,), (0,)), ((), ())), preferred_element_type=F32)
            state_ref[slot] = state * chunk_decay[hd][direction] + pushed

    def scan(hd):
        for t in range(nc):
            scan_step(t, FWD, hd, update=t < nc - 1)
            scan_step(nc - 1 - t, BWD, hd, update=t < nc - 1)

    def mixed(c, hd):
        shift = coef_shift(c)
        q, k, v = load(q_ref, c, qk_cols(hd)), load(k_ref, c, qk_cols(hd)), load(v_ref, c, v_cols(hd))
        s = lax.dot_general(q, k, (((1,), (1,)), ((), ())), preferred_element_type=F32)
        qf = q.astype(F32)
        lhs = jnp.concatenate([(s * coef_ref[hd, COEF_DECAY]).astype(BF16),
                               (qf * coef_ref[hd, COEF_Q_FWD + shift]).astype(BF16),
                               (qf * coef_ref[hd, COEF_Q_BWD + shift]).astype(BF16)], axis=1)
        return lhs, jnp.concatenate([v, stack_ref[stack_rows(c, hd), :]], axis=0)

    for hd in range(RET_HEADS):
        scan(hd)
    items = [(c, hd) for hd in range(RET_HEADS) for c in range(nc)]
    nxt = mixed(*items[0])
    for idx, (c, hd) in enumerate(items):
        lhs, rhs = nxt
        if idx + 1 < len(items):
            nxt = mixed(*items[idx + 1])
        o = _rms_rows(jnp.dot(lhs, rhs, preferred_element_type=F32))
        g = load(g_ref, c, v_cols(hd)).astype(F32)
        o = (o * (g * _sigmoid(g))).astype(o_ref.dtype)
        if c == 0:
            o_ref[0, s_rows:s_rows + N_META, v_cols(hd)] = o[:N_META]
        else:
            o_ref[0, (c - 1) * BLOCK:c * BLOCK, v_cols(hd)] = o


def _retention(log_gamma, rq, rk, rv, rg):
    b, lh, _ = rq.shape
    nc = (lh - N_META) // BLOCK + 1

    def seq(w):
        return pl.BlockSpec((1, lh, w), lambda i, lg: (i, 0, 0))

    return pl.pallas_call(
        _retention_kernel,
        grid_spec=pltpu.PrefetchScalarGridSpec(
            num_scalar_prefetch=1,
            grid=(b,),
            in_specs=[seq(RET_QK_W), seq(RET_QK_W), seq(RET_V_W), seq(RET_V_W)],
            out_specs=seq(RET_V_W),
            scratch_shapes=[pltpu.VMEM((RET_HEADS, N_COEF, BLOCK, BLOCK), F32),
                            pltpu.VMEM((2 * RET_HEADS, RET_QK_DIM, RET_V_DIM), F32),
                            pltpu.VMEM((nc * RET_HEADS * 2 * BLOCK, RET_V_DIM), BF16)],
        ),
        out_shape=jax.ShapeDtypeStruct((b, lh, RET_V_W), BF16),
        compiler_params=_params(1),
        name="retention",
    )(log_gamma, rq, rk, rv, rg)


def _attention_kernel(q_ref, k_ref, v_ref, o_ref):
    tq = q_ref.shape[1]
    lh = k_ref.shape[1]
    lp = lh + PAD
    sub = _row_tile(tq, ATT_SUB_ROWS)
    key = lax.broadcasted_iota(jnp.int32, (1, lp), 1)
    zeros = jnp.zeros((PAD, ATT_HEAD_DIM), k_ref.dtype)
    k = jnp.concatenate([k_ref[0], zeros], axis=0)
    v1 = jnp.concatenate([jnp.concatenate([v_ref[0], zeros], axis=0),
                          jnp.ones((lp, LANES), v_ref.dtype)], axis=1)
    chains = [(slice(r0, r0 + sub), slice(g * LANES, (g + 1) * LANES))
              for r0 in range(0, tq, sub) for g in range(ATT_GROUP)]

    def scores(c):
        rows, cols = chains[c]
        s = lax.dot_general(q_ref[0, rows, cols], k, (((1,), (1,)), ((), ())),
                            preferred_element_type=F32)
        return jnp.where(key < lh, s, MASK_VALUE)

    s_next = scores(0)
    for c, (rows, cols) in enumerate(chains):
        s = s_next
        if c + 1 < len(chains):
            s_next = scores(c + 1)
        p = jnp.exp2(s - jnp.max(s, axis=-1, keepdims=True))
        ov = jnp.dot(p.astype(BF16), v1, preferred_element_type=F32)
        o = ov[:, :ATT_HEAD_DIM] * (1.0 / ov[:, ATT_HEAD_DIM:])
        o_ref[0, rows, cols] = o.astype(o_ref.dtype)


def _attention(aq, ak, av):
    b, lh, _ = aq.shape
    tq = min(ATT_Q_ROWS, lh)
    return pl.pallas_call(
        _attention_kernel,
        grid=(b, ATT_KV_HEADS, pl.cdiv(lh, tq)),
        in_specs=[pl.BlockSpec((1, tq, ATT_GROUP_W), lambda i, j, t: (i, t, j)),
                  pl.BlockSpec((1, lh, ATT_HEAD_DIM), lambda i, j, t: (i, 0, j)),
                  pl.BlockSpec((1, lh, ATT_HEAD_DIM), lambda i, j, t: (i, 0, j))],
        out_specs=pl.BlockSpec((1, tq, ATT_GROUP_W), lambda i, j, t: (i, t, j)),
        out_shape=jax.ShapeDtypeStruct((b, lh, ATT_Q_W), BF16),
        compiler_params=_params(3),
        name="attention",
    )(aq, ak, av)


def _mix_out_kernel(rows, *refs):
    row_refs, refs = refs[:rows.n_refs()], refs[rows.n_refs():]
    ret_ref, att_ref, gr_ref, ga_ref, wr_ref, wa_ref, wo_ref, o_ref = refs
    r = jnp.dot(ret_ref[0], wr_ref[...], preferred_element_type=F32)
    a = jnp.dot(att_ref[0], wa_ref[...], preferred_element_type=F32)
    merged = _sigmoid(gr_ref[0].astype(F32)) * r + _sigmoid(ga_ref[0].astype(F32)) * a
    o_ref[0] = rows.load(row_refs) + jnp.dot(merged.astype(BF16), wo_ref[...],
                                             preferred_element_type=F32)


def _mix_out(rows, h, xs, meta, ret, att, gr, ga, w_ret_o, w_att_o, w_out, b):
    tm, nt = rows.tm, rows.nt
    tile = pl.BlockSpec((1, tm, D_MODEL), lambda i, j: (i, j, 0))
    w = _resident((D_MODEL, D_MODEL))
    return pl.pallas_call(
        functools.partial(_mix_out_kernel, rows),
        grid=(b, nt),
        in_specs=rows.specs() + [tile] * 4 + [w] * 3,
        out_specs=tile,
        out_shape=jax.ShapeDtypeStruct((b, nt * tm, D_MODEL), F32),
        input_output_aliases={} if rows.first_layer else {0: 0},
        compiler_params=_params(2),
        name="mix_out",
    )(*_row_inputs(rows, h, xs, meta), ret, att, gr, ga, w_ret_o, w_att_o, w_out)


def _ffn_kernel(is_final, h_ref, gain_ref, wi_ref, wo_ref, *rest):
    x = h_ref[0]
    xg = (x * gain_ref[...]).astype(BF16)
    inv_rms = lax.rsqrt(jnp.mean(x * x, axis=-1, keepdims=True) + NORM_EPS)
    def pair(j):
        lo = j * FFN_CHUNK
        return (jnp.dot(xg, wi_ref[:, lo:lo + FFN_CHUNK], preferred_element_type=F32),
                jnp.dot(xg, wi_ref[:, D_FF + lo:D_FF + lo + FFN_CHUNK],
                        preferred_element_type=F32))

    acts = []
    nxt = pair(0)
    for j in range(D_FF // FFN_CHUNK):
        a, u = nxt
        if j + 1 < D_FF // FFN_CHUNK:
            nxt = pair(j + 1)
        a = a * inv_rms
        acts.append((a * _sigmoid(a) * (u * inv_rms)).astype(BF16))
    act = jnp.concatenate(acts, axis=1)
    out = x + jnp.dot(act, wo_ref[...], preferred_element_type=F32)
    if not is_final:
        (o_ref,) = rest
        o_ref[0] = out
    else:
        fgain_ref, y_ref = rest
        y_ref[0] = _rms_rows(out) * fgain_ref[...]


def _ffn(h, gain, w_ffn_in, w_ffn_out, tm, final=None):
    b, lh, _ = h.shape
    nt = lh // tm
    tile = pl.BlockSpec((1, tm, D_MODEL), lambda i, j: (i, j, 0))
    weights = [_resident((1, D_MODEL)), _resident((D_MODEL, 2 * D_FF)), _resident((D_FF, D_MODEL))]
    if final is None:
        return pl.pallas_call(
            functools.partial(_ffn_kernel, False),
            grid=(b, nt),
            in_specs=[tile] + weights,
            out_specs=tile,
            out_shape=jax.ShapeDtypeStruct(h.shape, F32),
            input_output_aliases={0: 0},
            compiler_params=_params(2),
            name="ffn",
        )(h, gain, w_ffn_in, w_ffn_out)
    fgain, counts, s = final
    outs, first = [], 0
    for count in counts:
        outs.append(pl.pallas_call(
            functools.partial(_ffn_kernel, True),
            grid=(count, nt),
            in_specs=[pl.BlockSpec((1, tm, D_MODEL), lambda i, j, first=first: (i + first, j, 0))]
            + weights + [_resident((1, D_MODEL))],
            out_specs=tile,
            out_shape=jax.ShapeDtypeStruct((count, s, D_MODEL), F32),
            compiler_params=_params(2),
            name="ffn_final",
        )(h, gain, w_ffn_in, w_ffn_out, fgain))
        first += count
    return outs


def _trunk(xs, meta_tokens, norm_mix, w_in, ret_decay, q_norm, k_norm, w_ret_o, w_att_o, w_out,
           norm_ffn, w_ffn_in, w_ffn_out, norm_final):
    assert len(xs) == 2
    s = xs[0].shape[1]
    assert all(x.shape[1:] == (s, D_MODEL) for x in xs) and s % GRID_W == 0 and s % BLOCK == 0
    lh = s + N_META
    counts = tuple(x.shape[0] for x in xs)
    b = sum(counts)
    tm = _row_tile(lh)
    nt = lh // tm
    assert s > (nt - 1) * tm, "the meta tokens must sit in the last row tile"
    meta = meta_tokens.astype(F32)
    tabs = _rotary_tables(s)
    gamma = 1.0 - jnp.exp2(-ret_decay.astype(F32))
    log_gamma = jnp.log(gamma)
    h = None
    for l in range(DEPTH):
        rows = _Rows(tm, nt, s, counts, first_layer=l == 0)
        rq, rk, rv, rg, aq, ak, av, gr, ga = _in_proj(
            rows, h, xs, meta, norm_mix[l][None], w_in[l].astype(BF16), tabs, q_norm[l][None],
            k_norm[l][None], b)
        ret = _retention(log_gamma[l], rq, rk, rv, rg)
        att = _attention(aq, ak, av)
        h = _mix_out(rows, h, xs, meta, ret, att, gr, ga, w_ret_o[l].astype(BF16),
                     w_att_o[l].astype(BF16), w_out[l].astype(BF16), b)
        final = (norm_final[None], counts, s) if l == DEPTH - 1 else None
        h = _ffn(h, norm_ffn[l][None], w_ffn_in[l].astype(BF16), w_ffn_out[l].astype(BF16), tm,
                 final)
    return h


def kernel(x_prompt, x_sample, meta_tokens, norm_mix, w_in, ret_decay, q_norm, k_norm, w_ret_o,
           w_att_o, w_out, norm_ffn, w_ffn_in, w_ffn_out, norm_final):
    y_prompt, y_sample = _trunk([x_prompt, x_sample], meta_tokens, norm_mix, w_in, ret_decay,
                                q_norm, k_norm, w_ret_o, w_att_o, w_out, norm_ffn, w_ffn_in,
                                w_ffn_out, norm_final)
    return (y_prompt, y_sample)
```

```python
import functools
import math

import jax
import jax.numpy as jnp
from jax import lax
from jax.experimental import pallas as pl
from jax.experimental.pallas import tpu as pltpu

D_MODEL = 1024
DEPTH = 2
N_META = 16
GRID_W = 64
BLOCK = 128
PAD = BLOCK - N_META
RET_HEADS = 4
RET_QK_DIM = 128
RET_V_DIM = 256
ATT_HEADS = 8
ATT_KV_HEADS = 2
ATT_GROUP = ATT_HEADS // ATT_KV_HEADS
ATT_HEAD_DIM = 128
D_FF = ((8 * D_MODEL + 3 * 256 - 1) // (3 * 256)) * 256
ROPE_BASE = 10000.0
NORM_EPS = 1e-6
RET_QK_W = RET_HEADS * RET_QK_DIM
RET_V_W = RET_HEADS * RET_V_DIM
ATT_Q_W = ATT_HEADS * ATT_HEAD_DIM
ATT_KV_W = ATT_KV_HEADS * ATT_HEAD_DIM
ATT_GROUP_W = ATT_GROUP * ATT_HEAD_DIM
IN_NAMES = ("rq", "rk", "rv", "rg", "aq", "ak", "av", "gr", "ga")
IN_SPLITS = (RET_QK_W, RET_QK_W, RET_V_W, RET_V_W, ATT_Q_W, ATT_KV_W, ATT_KV_W, D_MODEL, D_MODEL)
IN_WIDTH = sum(IN_SPLITS)

LANES = 128
BF16_SUBLANES = 16
VMEM_BYTES_V7X = 64 * 1024 * 1024
VMEM_LIMIT = VMEM_BYTES_V7X * 7 // 8
ROW_TILE_TARGET = 688
ATT_Q_ROWS = 1056
ATT_SUB_ROWS = 352
MXU_WIDTH_V7X = 256
FFN_CHUNK = MXU_WIDTH_V7X
MIX_INPUT_BUFFERS = 3
MASK_VALUE = -1e30
ATT_Q_SCALE = ATT_HEAD_DIM ** -0.5 * math.log2(math.e)

F32 = jnp.float32
BF16 = jnp.bfloat16

TAB_RC, TAB_RS, TAB_AC, TAB_ASP, TAB_ASM = range(5)
TAB_W = 5 * LANES


def _row_tile(rows, target=None):
    target = ROW_TILE_TARGET if target is None else target
    best = None
    for d in range(BF16_SUBLANES, min(rows, target) + 1, BF16_SUBLANES):
        if rows % d == 0:
            best = d
    assert best is not None, rows
    return best


def _params(n_axes):
    return pltpu.CompilerParams(dimension_semantics=("arbitrary",) * n_axes,
                                vmem_limit_bytes=VMEM_LIMIT)


def _resident(shape):
    return pl.BlockSpec(shape, lambda *_: (0,) * len(shape), pipeline_mode=pl.Buffered(1))


def _rms_rows(x):
    return x * lax.rsqrt(jnp.mean(x * x, axis=-1, keepdims=True) + NORM_EPS)


def _sigmoid(x):
    return 1.0 / (1.0 + jnp.exp(-x))


def _rotary_tables(s):
    r = jnp.arange(s + N_META, dtype=jnp.int32)
    is_tok = r < s
    t = jnp.where(is_tok, r + N_META, r - s)
    ret_inv = ROPE_BASE ** (-jnp.linspace(0.0, 1.0, RET_QK_DIM // 2, dtype=F32))
    ret_ang = t.astype(F32)[:, None] * ret_inv[None, :]
    rc, rs = jnp.cos(ret_ang), jnp.sin(ret_ang)
    row = jnp.where(is_tok, r // GRID_W, 0).astype(F32)
    col = jnp.where(is_tok, r % GRID_W, 0).astype(F32)
    ax_half = ATT_HEAD_DIM // 2
    ax_inv = ROPE_BASE ** (-jnp.arange(ax_half // 2, dtype=F32) * 2.0 / ax_half)
    ra, ca = row[:, None] * ax_inv[None, :], col[:, None] * ax_inv[None, :]
    z = jnp.zeros_like(ra)
    return jnp.concatenate([
        rc, rc,
        -rs, rs,
        jnp.cos(ra), jnp.cos(ra), jnp.cos(ca), jnp.cos(ca),
        -jnp.sin(ra), z, -jnp.sin(ca), z,
        z, jnp.sin(ra), z, jnp.sin(ca),
    ], axis=1)


class _Rows:
    def __init__(self, tm, nt, s, counts, first_layer):
        self.tm, self.nt, self.s, self.counts, self.first_layer = tm, nt, s, counts, first_layer

    def specs(self):
        tm, nt = self.tm, self.nt
        if not self.first_layer:
            return [pl.BlockSpec((1, tm, D_MODEL), lambda i, j: (i, j, 0))]
        n0 = self.counts[0]

        def first(i, j):
            mine = i < n0
            return (jnp.where(mine, i, n0 - 1), jnp.where(mine, j, nt - 1), 0)

        def second(i, j):
            mine = i >= n0
            return (jnp.where(mine, i - n0, 0), jnp.where(mine, j, 0), 0)

        return [pl.BlockSpec((1, tm, D_MODEL), first), pl.BlockSpec((1, tm, D_MODEL), second),
                _resident((N_META, D_MODEL))]

    def n_refs(self):
        return 3 if self.first_layer else 1

    def load(self, refs):
        if not self.first_layer:
            return refs[0][0]
        xa_ref, xb_ref, meta_ref = refs
        i, j = pl.program_id(0), pl.program_id(1)
        x = jnp.where(i < self.counts[0], xa_ref[0], xb_ref[0])
        meta_at = self.s - (self.nt - 1) * self.tm
        assert meta_at + N_META == self.tm
        meta_tile = jnp.concatenate([jnp.zeros((meta_at, D_MODEL), F32), meta_ref[...]], axis=0)
        row = j * self.tm + lax.broadcasted_iota(jnp.int32, (self.tm, 1), 0)
        return jnp.where(row < self.s, x, meta_tile)


def _row_inputs(rows, h, xs, meta):
    return [xs[0], xs[1], meta] if rows.first_layer else [h]


def _in_proj_kernel(rows, *refs):
    row_refs, refs = refs[:rows.n_refs()], refs[rows.n_refs():]
    gain_ref, w_ref, tab_ref, qg_ref, kg_ref = refs[:5]
    outs = dict(zip(IN_NAMES, refs[5:]))
    offsets = {}
    lo = 0
    for name, width in zip(IN_NAMES, IN_SPLITS):
        offsets[name] = (lo, width)
        lo += width

    def tab(i):
        return tab_ref[:, i * LANES:(i + 1) * LANES]

    def ret_rope(x, scale):
        y = x * tab(TAB_RC) + pltpu.roll(x, RET_QK_DIM // 2, 1) * tab(TAB_RS)
        return y if scale is None else y * scale

    def axial(x, gain, scale):
        y = _rms_rows(x) * gain
        y = (y * tab(TAB_AC) + pltpu.roll(y, LANES - 32, 1) * tab(TAB_ASP)
             + pltpu.roll(y, 32, 1) * tab(TAB_ASM))
        return y if scale is None else y * scale

    x = rows.load(row_refs)
    xg = (x * gain_ref[...]).astype(BF16)
    inv_rms = lax.rsqrt(jnp.mean(x * x, axis=-1, keepdims=True) + NORM_EPS)

    def proj(name):
        lo, width = offsets[name]
        return jnp.dot(xg, w_ref[:, lo:lo + width], preferred_element_type=F32) * inv_rms

    def heads(name, fn):
        y = proj(name)
        for hd in range(y.shape[1] // LANES):
            sl = slice(hd * LANES, (hd + 1) * LANES)
            outs[name][0, :, sl] = fn(y[:, sl]).astype(outs[name].dtype)

    def plain(name):
        outs[name][0] = proj(name).astype(outs[name].dtype)

    heads("aq", lambda y: axial(y, qg_ref[...], ATT_Q_SCALE))
    plain("rv")
    heads("ak", lambda y: axial(y, kg_ref[...], None))
    plain("rg")
    heads("rq", lambda y: ret_rope(y, None))
    plain("gr")
    heads("rk", lambda y: ret_rope(y, RET_QK_DIM ** -0.5))
    plain("ga")
    plain("av")


def _in_proj(rows, h, xs, meta, gain, w_in, tabs, q_gain, k_gain, b):
    tm, nt = rows.tm, rows.nt

    def out_rows(w):
        return pl.BlockSpec((1, tm, w), lambda i, j: (i, j, 0))

    return pl.pallas_call(
        functools.partial(_in_proj_kernel, rows),
        grid=(b, nt),
        in_specs=rows.specs() + [_resident((1, D_MODEL)), _resident((D_MODEL, IN_WIDTH)),
                                 pl.BlockSpec((tm, TAB_W), lambda i, j: (j, 0)),
                                 _resident((1, LANES)), _resident((1, LANES))],
        out_specs=[out_rows(w) for w in IN_SPLITS],
        out_shape=[jax.ShapeDtypeStruct((b, nt * tm, w), BF16) for w in IN_SPLITS],
        compiler_params=_params(2),
        name="in_proj",
    )(*_row_inputs(rows, h, xs, meta), gain, w_in, tabs, q_gain, k_gain)


(COEF_DECAY, COEF_Q_FWD, COEF_Q_BWD, COEF_K_FWD, COEF_K_BWD,
 COEF_Q_FWD_META, COEF_Q_BWD_META, COEF_K_FWD_META, COEF_K_BWD_META) = range(9)
N_COEF = 9
FWD, BWD = 0, 1


def _retention_kernel(lg_ref, q_ref, k_ref, v_ref, g_ref, o_ref, coef_ref, state_ref, stack_ref):
    s_rows = q_ref.shape[1] - N_META
    nc = s_rows // BLOCK + 1
    row = lax.broadcasted_iota(jnp.int32, (BLOCK, BLOCK), 0).astype(F32)
    col = lax.broadcasted_iota(jnp.int32, (BLOCK, BLOCK), 1).astype(F32)
    rel = row - col
    meta_pos = jnp.minimum(row + PAD, BLOCK - 1.0)
    chunk_decay = []
    for hd in range(RET_HEADS):
        log_f, log_b = lg_ref[FWD, hd], lg_ref[BWD, hd]
        coef_ref[hd, COEF_DECAY] = jnp.where(rel >= 0, jnp.exp(log_f * jnp.maximum(rel, 0.0)),
                                             jnp.exp(log_b * jnp.maximum(-rel, 0.0)))
        for pos, shift in ((row, 0), (meta_pos, COEF_Q_FWD_META - COEF_Q_FWD)):
            coef_ref[hd, COEF_Q_FWD + shift] = jnp.exp(log_f * (pos + 1.0))
            coef_ref[hd, COEF_Q_BWD + shift] = jnp.exp(log_b * (BLOCK - pos))
            coef_ref[hd, COEF_K_FWD + shift] = jnp.exp(log_f * (BLOCK - 1.0 - pos))
            coef_ref[hd, COEF_K_BWD + shift] = jnp.exp(log_b * pos)
        ones_row = jnp.ones((1, RET_V_DIM), F32)
        chunk_decay.append((jnp.exp(log_f * BLOCK * ones_row), jnp.exp(log_b * BLOCK * ones_row)))
    state_ref[...] = jnp.zeros_like(state_ref)

    def coef_shift(c):
        return COEF_Q_FWD_META - COEF_Q_FWD if c == 0 else 0

    def load(ref, c, cols):
        if c == 0:
            meta = ref[0, s_rows:s_rows + N_META, cols]
            return jnp.concatenate([meta, jnp.zeros((PAD, meta.shape[1]), meta.dtype)], axis=0)
        return ref[0, (c - 1) * BLOCK:c * BLOCK, cols]

    def qk_cols(hd):
        return slice(hd * RET_QK_DIM, (hd + 1) * RET_QK_DIM)

    def v_cols(hd):
        return slice(hd * RET_V_DIM, (hd + 1) * RET_V_DIM)

    def stack_rows(c, hd, direction=None):
        base = (c * RET_HEADS + hd) * 2 * BLOCK
        if direction is None:
            return slice(base, base + 2 * BLOCK)
        return slice(base + direction * BLOCK, base + (direction + 1) * BLOCK)

    def scan_step(c, direction, hd, update):
        shift = coef_shift(c)
        slot = direction * RET_HEADS + hd
        state = state_ref[slot]
        stack_ref[stack_rows(c, hd, direction), :] = state.astype(stack_ref.dtype)
        if update:
            k = (load(k_ref, c, qk_cols(hd)).astype(F32)
                 * coef_ref[hd, COEF_K_FWD + direction + shift])
            pushed = lax.dot_general(k.astype(BF16), load(v_ref, c, v_cols(hd)),
                                     (((0,), (0,)), ((), ())), preferred_element_type=F32)
            state_ref[slot] = state * chunk_decay[hd][direction] + pushed

    def scan(hd):
        for t in range(nc):
            scan_step(t, FWD, hd, update=t < nc - 1)
            scan_step(nc - 1 - t, BWD, hd, update=t < nc - 1)

    def mixed(c, hd):
        shift = coef_shift(c)
        q, k, v = load(q_ref, c, qk_cols(hd)), load(k_ref, c, qk_cols(hd)), load(v_ref, c, v_cols(hd))
        s = lax.dot_general(q, k, (((1,), (1,)), ((), ())), preferred_element_type=F32)
        qf = q.astype(F32)
        lhs = jnp.concatenate([(s * coef_ref[hd, COEF_DECAY]).astype(BF16),
                               (qf * coef_ref[hd, COEF_Q_FWD + shift]).astype(BF16),
                               (qf * coef_ref[hd, COEF_Q_BWD + shift]).astype(BF16)], axis=1)
        return lhs, jnp.concatenate([v, stack_ref[stack_rows(c, hd), :]], axis=0)

    for hd in range(RET_HEADS):
        scan(hd)
    items = [(c, hd) for hd in range(RET_HEADS) for c in range(nc)]
    nxt = mixed(*items[0])
    for idx, (c, hd) in enumerate(items):
        lhs, rhs = nxt
        if idx + 1 < len(items):
            nxt = mixed(*items[idx + 1])
        o = _rms_rows(jnp.dot(lhs, rhs, preferred_element_type=F32))
        g = load(g_ref, c, v_cols(hd)).astype(F32)
        o = (o * (g * _sigmoid(g))).astype(o_ref.dtype)
        if c == 0:
            o_ref[0, s_rows:s_rows + N_META, v_cols(hd)] = o[:N_META]
        else:
            o_ref[0, (c - 1) * BLOCK:c * BLOCK, v_cols(hd)] = o


def _retention(log_gamma, rq, rk, rv, rg):
    b, lh, _ = rq.shape
    nc = (lh - N_META) // BLOCK + 1

    def seq(w):
        return pl.BlockSpec((1, lh, w), lambda i, lg: (i, 0, 0))

    return pl.pallas_call(
        _retention_kernel,
        grid_spec=pltpu.PrefetchScalarGridSpec(
            num_scalar_prefetch=1,
            grid=(b,),
            in_specs=[seq(RET_QK_W), seq(RET_QK_W), seq(RET_V_W), seq(RET_V_W)],
            out_specs=seq(RET_V_W),
            scratch_shapes=[pltpu.VMEM((RET_HEADS, N_COEF, BLOCK, BLOCK), F32),
                            pltpu.VMEM((2 * RET_HEADS, RET_QK_DIM, RET_V_DIM), F32),
                            pltpu.VMEM((nc * RET_HEADS * 2 * BLOCK, RET_V_DIM), BF16)],
        ),
        out_shape=jax.ShapeDtypeStruct((b, lh, RET_V_W), BF16),
        compiler_params=_params(1),
        name="retention",
    )(log_gamma, rq, rk, rv, rg)


def _attention_kernel(q_ref, k_ref, v_ref, o_ref):
    tq = q_ref.shape[1]
    lh = k_ref.shape[1]
    lp = lh + PAD
    sub = _row_tile(tq, ATT_SUB_ROWS)
    key = lax.broadcasted_iota(jnp.int32, (1, lp), 1)
    zeros = jnp.zeros((PAD, ATT_HEAD_DIM), k_ref.dtype)
    k = jnp.concatenate([k_ref[0], zeros], axis=0)
    v1 = jnp.concatenate([jnp.concatenate([v_ref[0], zeros], axis=0),
                          jnp.ones((lp, LANES), v_ref.dtype)], axis=1)
    chains = [(slice(r0, r0 + sub), slice(g * LANES, (g + 1) * LANES))
              for r0 in range(0, tq, sub) for g in range(ATT_GROUP)]

    def scores(c):
        rows, cols = chains[c]
        s = lax.dot_general(q_ref[0, rows, cols], k, (((1,), (1,)), ((), ())),
                            preferred_element_type=F32)
        return jnp.where(key < lh, s, MASK_VALUE)

    s_next = scores(0)
    for c, (rows, cols) in enumerate(chains):
        s = s_next
        if c + 1 < len(chains):
            s_next = scores(c + 1)
        p = jnp.exp2(s - jnp.max(s, axis=-1, keepdims=True))
        ov = jnp.dot(p.astype(BF16), v1, preferred_element_type=F32)
        o = ov[:, :ATT_HEAD_DIM] * (1.0 / ov[:, ATT_HEAD_DIM:])
        o_ref[0, rows, cols] = o.astype(o_ref.dtype)


def _attention(aq, ak, av):
    b, lh, _ = aq.shape
    tq = min(ATT_Q_ROWS, lh)
    return pl.pallas_call(
        _attention_kernel,
        grid=(b, ATT_KV_HEADS, pl.cdiv(lh, tq)),
        in_specs=[pl.BlockSpec((1, tq, ATT_GROUP_W), lambda i, j, t: (i, t, j)),
                  pl.BlockSpec((1, lh, ATT_HEAD_DIM), lambda i, j, t: (i, 0, j)),
                  pl.BlockSpec((1, lh, ATT_HEAD_DIM), lambda i, j, t: (i, 0, j))],
        out_specs=pl.BlockSpec((1, tq, ATT_GROUP_W), lambda i, j, t: (i, t, j)),
        out_shape=jax.ShapeDtypeStruct((b, lh, ATT_Q_W), BF16),
        compiler_params=_params(3),
        name="attention",
    )(aq, ak, av)


def _mix_out_kernel(rows, *refs):
    row_refs, refs = refs[:rows.n_refs()], refs[rows.n_refs():]
    ret_ref, att_ref, gr_ref, ga_ref, wr_ref, wa_ref, wo_ref, o_ref = refs
    r = jnp.dot(ret_ref[0], wr_ref[...], preferred_element_type=F32)
    a = jnp.dot(att_ref[0], wa_ref[...], preferred_element_type=F32)
    merged = _sigmoid(gr_ref[0].astype(F32)) * r + _sigmoid(ga_ref[0].astype(F32)) * a
    o_ref[0] = rows.load(row_refs) + jnp.dot(merged.astype(BF16), wo_ref[...],
                                             preferred_element_type=F32)


def _mix_out_deep(rows, h, ret, att, gr, ga, w_ret_o, w_att_o, w_out, b):
    tm, nt = rows.tm, rows.nt

    def streamed(buffers):
        return pl.BlockSpec((1, tm, D_MODEL), lambda i, j: (i, j, 0),
                            pipeline_mode=pl.Buffered(buffers))

    def outer(h_hbm, ret_hbm, att_hbm, gr_hbm, ga_hbm, wr_ref, wa_ref, wo_ref, o_hbm):
        def step(h_ref, ret_ref, att_ref, gr_ref, ga_ref, o_ref):
            _mix_out_kernel(rows, h_ref, ret_ref, att_ref, gr_ref, ga_ref, wr_ref, wa_ref, wo_ref,
                            o_ref)

        pltpu.emit_pipeline(step, grid=(b, nt),
                            in_specs=[streamed(MIX_INPUT_BUFFERS)] * 5,
                            out_specs=[streamed(2)])(h_hbm, ret_hbm, att_hbm, gr_hbm, ga_hbm, o_hbm)

    any_space = pl.BlockSpec(memory_space=pl.ANY)
    whole = pl.BlockSpec(memory_space=pltpu.VMEM)
    return pl.pallas_call(
        outer,
        in_specs=[any_space] * 5 + [whole] * 3,
        out_specs=any_space,
        out_shape=jax.ShapeDtypeStruct((b, nt * tm, D_MODEL), F32),
        compiler_params=pltpu.CompilerParams(vmem_limit_bytes=VMEM_LIMIT),
        name="mix_out_deep",
    )(h, ret, att, gr, ga, w_ret_o, w_att_o, w_out)


def _mix_out(rows, h, xs, meta, ret, att, gr, ga, w_ret_o, w_att_o, w_out, b):
    tm, nt = rows.tm, rows.nt
    tile = pl.BlockSpec((1, tm, D_MODEL), lambda i, j: (i, j, 0))
    if not rows.first_layer:
        return _mix_out_deep(rows, h, ret, att, gr, ga, w_ret_o, w_att_o, w_out, b)
    w = _resident((D_MODEL, D_MODEL))
    return pl.pallas_call(
        functools.partial(_mix_out_kernel, rows),
        grid=(b, nt),
        in_specs=rows.specs() + [tile] * 4 + [w] * 3,
        out_specs=tile,
        out_shape=jax.ShapeDtypeStruct((b, nt * tm, D_MODEL), F32),
        input_output_aliases={} if rows.first_layer else {0: 0},
        compiler_params=_params(2),
        name="mix_out",
    )(*_row_inputs(rows, h, xs, meta), ret, att, gr, ga, w_ret_o, w_att_o, w_out)


def _ffn_kernel(is_final, h_ref, gain_ref, wi_ref, wo_ref, *rest):
    x = h_ref[0]
    xg = (x * gain_ref[...]).astype(BF16)
    inv_rms = lax.rsqrt(jnp.mean(x * x, axis=-1, keepdims=True) + NORM_EPS)
    def pair(j):
        lo = j * FFN_CHUNK
        return (jnp.dot(xg, wi_ref[:, lo:lo + FFN_CHUNK], preferred_element_type=F32),
                jnp.dot(xg, wi_ref[:, D_FF + lo:D_FF + lo + FFN_CHUNK],
                        preferred_element_type=F32))

    acts = []
    nxt = pair(0)
    for j in range(D_FF // FFN_CHUNK):
        a, u = nxt
        if j + 1 < D_FF // FFN_CHUNK:
            nxt = pair(j + 1)
        a = a * inv_rms
        acts.append((a * _sigmoid(a) * (u * inv_rms)).astype(BF16))
    act = jnp.concatenate(acts, axis=1)
    out = x + jnp.dot(act, wo_ref[...], preferred_element_type=F32)
    if not is_final:
        (o_ref,) = rest
        o_ref[0] = out
    else:
        fgain_ref, y_ref = rest
        y_ref[0] = _rms_rows(out) * fgain_ref[...]


def _ffn(h, gain, w_ffn_in, w_ffn_out, tm, final=None):
    b, lh, _ = h.shape
    nt = lh // tm
    tile = pl.BlockSpec((1, tm, D_MODEL), lambda i, j: (i, j, 0))
    weights = [_resident((1, D_MODEL)), _resident((D_MODEL, 2 * D_FF)), _resident((D_FF, D_MODEL))]
    if final is None:
        return pl.pallas_call(
            functools.partial(_ffn_kernel, False),
            grid=(b, nt),
            in_specs=[tile] + weights,
            out_specs=tile,
            out_shape=jax.ShapeDtypeStruct(h.shape, F32),
            input_output_aliases={0: 0},
            compiler_params=_params(2),
            name="ffn",
        )(h, gain, w_ffn_in, w_ffn_out)
    fgain, counts, s = final
    outs, first = [], 0
    for count in counts:
        outs.append(pl.pallas_call(
            functools.partial(_ffn_kernel, True),
            grid=(count, nt),
            in_specs=[pl.BlockSpec((1, tm, D_MODEL), lambda i, j, first=first: (i + first, j, 0))]
            + weights + [_resident((1, D_MODEL))],
            out_specs=tile,
            out_shape=jax.ShapeDtypeStruct((count, s, D_MODEL), F32),
            compiler_params=_params(2),
            name="ffn_final",
        )(h, gain, w_ffn_in, w_ffn_out, fgain))
        first += count
    return outs


def _trunk(xs, meta_tokens, norm_mix, w_in, ret_decay, q_norm, k_norm, w_ret_o, w_att_o, w_out,
           norm_ffn, w_ffn_in, w_ffn_out, norm_final):
    assert len(xs) == 2
    s = xs[0].shape[1]
    assert all(x.shape[1:] == (s, D_MODEL) for x in xs) and s % GRID_W == 0 and s % BLOCK == 0
    lh = s + N_META
    counts = tuple(x.shape[0] for x in xs)
    b = sum(counts)
    tm = _row_tile(lh)
    nt = lh // tm
    assert s > (nt - 1) * tm, "the meta tokens must sit in the last row tile"
    meta = meta_tokens.astype(F32)
    tabs = _rotary_tables(s)
    gamma = 1.0 - jnp.exp2(-ret_decay.astype(F32))
    log_gamma = jnp.log(gamma)
    h = None
    for l in range(DEPTH):
        rows = _Rows(tm, nt, s, counts, first_layer=l == 0)
        rq, rk, rv, rg, aq, ak, av, gr, ga = _in_proj(
            rows, h, xs, meta, norm_mix[l][None], w_in[l].astype(BF16), tabs, q_norm[l][None],
            k_norm[l][None], b)
        ret = _retention(log_gamma[l], rq, rk, rv, rg)
        att = _attention(aq, ak, av)
        h = _mix_out(rows, h, xs, meta, ret, att, gr, ga, w_ret_o[l].astype(BF16),
                     w_att_o[l].astype(BF16), w_out[l].astype(BF16), b)
        final = (norm_final[None], counts, s) if l == DEPTH - 1 else None
        h = _ffn(h, norm_ffn[l][None], w_ffn_in[l].astype(BF16), w_ffn_out[l].astype(BF16), tm,
                 final)
    return h


def kernel(x_prompt, x_sample, meta_tokens, norm_mix, w_in, ret_decay, q_norm, k_norm, w_ret_o,
           w_att_o, w_out, norm_ffn, w_ffn_in, w_ffn_out, norm_final):
    y_prompt, y_sample = _trunk([x_prompt, x_sample], meta_tokens, norm_mix, w_in, ret_decay,
                                q_norm, k_norm, w_ret_o, w_att_o, w_out, norm_ffn, w_ffn_in,
                                w_ffn_out, norm_final)
    return (y_prompt, y_sample)
```
